```python
import math
import jax, jax.numpy as jnp
from jax import lax
import numpy as np

D_MODEL = 1024
BATCH = 1
SEQ = 16384
DEPTH = 4

GRID_W = 64
CTX_LEN = 256
N_MIXERS = 3
NORM_EPS = 1e-6
ROPE_THETA = 10000.0
Q_BLOCK = 128

A_HEADS = 8
A_HEAD_DIM = 64
A_WIDTH = A_HEADS * 2 * A_HEAD_DIM
B_HEADS = 16
B_HEAD_DIM = 64
B_WIDTH = B_HEADS * B_HEAD_DIM
NA_KH = 8
NA_KW = 16
NA_ROW_BLOCK = 2
C_HEADS = 16
C_KV_HEADS = 4
C_GROUP = C_HEADS // C_KV_HEADS
C_HEAD_DIM = 64
C_WINDOW = 128
C_QKV = C_HEADS * C_HEAD_DIM + 2 * C_KV_HEADS * C_HEAD_DIM
FFN_HIDDEN = ((8 * D_MODEL + 3 * 256 - 1) // (3 * 256)) * 256
N_A = (DEPTH + 2) // 3
N_B = (DEPTH + 1) // 3
N_C = DEPTH // 3

kernel_name = "hybrid_diff_na_swa_dit_trunk"


def rmsnorm(x, g):
    xf = x.astype(jnp.float32)
    y = xf * lax.rsqrt(jnp.mean(xf * xf, axis=-1, keepdims=True) + NORM_EPS)
    return (y * g.astype(jnp.float32)).astype(x.dtype)


def grid_angles(n_tok, rot_dim):
    half = rot_dim // 2
    freqs = ROPE_THETA ** (-jnp.arange(0, half, 2, dtype=jnp.float32) / half)
    pos = jnp.arange(n_tok)
    rows = (pos // GRID_W).astype(jnp.float32)
    cols = (pos % GRID_W).astype(jnp.float32)
    return rows[:, None] * freqs, cols[:, None] * freqs


def rotate(x, ang):
    cos = jnp.cos(ang)[:, None, :].astype(x.dtype)
    sin = jnp.sin(ang)[:, None, :].astype(x.dtype)
    x1, x2 = jnp.split(x, 2, axis=-1)
    return jnp.concatenate([x1 * cos - x2 * sin, x2 * cos + x1 * sin], axis=-1)


def rope_2d(x, ang_row, ang_col):
    xr, xc = jnp.split(x, 2, axis=-1)
    return jnp.concatenate([rotate(xr, ang_row), rotate(xc, ang_col)], axis=-1)


def diff_attention(h_lat, h_ctx, wqkv, wo, lam, subln, lam_init, ang, with_ctx_out):
    B, S, _ = h_lat.shape
    H, d = A_HEADS, A_HEAD_DIM
    scale = d ** -0.5

    def project(h):
        n = h.shape[1]
        q, k, v = jnp.split(h @ wqkv, 3, axis=-1)
        return q.reshape(B, n, H, 2, d), k.reshape(B, n, H, 2, d), v.reshape(B, n, H, 2 * d)

    q, k, v = project(h_lat)
    qc, kc, vc = project(h_ctx)
    q = rope_2d(q.reshape(B, S, 2 * H, d), *ang).reshape(B, S, H, 2, d) * scale
    k = rope_2d(k.reshape(B, S, 2 * H, d), *ang).reshape(B, S, H, 2, d)
    lam_full = (jnp.exp(jnp.sum(lam[0] * lam[1])) - jnp.exp(jnp.sum(lam[2] * lam[3])) + lam_init).astype(jnp.float32)

    def attend(qb, kk, vv):
        s = jnp.einsum('bqhmd,bkhmd->bhmqk', qb, kk).astype(jnp.float32)
        p = jax.nn.softmax(s, axis=-1)
        a = (p[:, :, 0] - lam_full * p[:, :, 1]).astype(vv.dtype)
        return jnp.einsum('bhqk,bkhe->bqhe', a, vv)

    def finish(o):
        n = o.shape[1]
        return (rmsnorm(o, subln) * (1.0 - lam_init)).reshape(B, n, A_WIDTH) @ wo

    keys = jnp.concatenate([kc, k], axis=1)
    vals = jnp.concatenate([vc, v], axis=1)
    nb = S // Q_BLOCK
    qblk = q.reshape(B, nb, Q_BLOCK, H, 2, d).swapaxes(0, 1)
    o = lax.map(lambda qb: attend(qb, keys, vals), qblk)
    y_lat = finish(o.swapaxes(0, 1).reshape(B, S, H, 2 * d))
    y_ctx = finish(attend(qc * scale, kc, vc)) if with_ctx_out else None
    return y_lat, y_ctx


def neighbourhood_attention(h_lat, h_ctx, wqkv, wo, rpb, with_ctx_out):
    B, S, _ = h_lat.shape
    H, d = B_HEADS, B_HEAD_DIM
    scale = d ** -0.5
    rows = S // GRID_W
    kh = min(NA_KH, rows)
    kw = NA_KW

    def project(h):
        n = h.shape[1]
        q, k, v = jnp.split(h @ wqkv, 3, axis=-1)
        return q.reshape(B, n, H, d), k.reshape(B, n, H, d), v.reshape(B, n, H, d)

    q, k, v = project(h_lat)
    qc, kc, vc = project(h_ctx)
    q = q * scale

    r = jnp.arange(rows)
    cg = jnp.arange(GRID_W)
    krow = jnp.clip(r - kh // 2, 0, rows - kh)[:, None] + jnp.arange(kh)
    kcol = jnp.clip(cg - kw // 2, 0, GRID_W - kw)[:, None] + jnp.arange(kw)
    nk = kh * kw
    nbr = (krow[:, None, :, None] * GRID_W + kcol[None, :, None, :]).reshape(S, nk)
    drow = krow - r[:, None] + (NA_KH - 1)
    dcol = kcol - cg[:, None] + (NA_KW - 1)
    rel = (drow[:, None, :, None] * (2 * NA_KW - 1) + dcol[None, :, None, :]).reshape(S, nk)
    rpb_flat = rpb.reshape(H, -1).astype(jnp.float32)

    qb_len = NA_ROW_BLOCK * GRID_W
    nb = rows // NA_ROW_BLOCK

    def block(args):
        qb, idx, rl = args
        kg = jnp.take(k, idx, axis=1)
        vg = jnp.take(v, idx, axis=1)
        s_nb = jnp.einsum('bqhd,bqnhd->bhqn', qb, kg).astype(jnp.float32) + rpb_flat[:, rl]
        s_cx = jnp.einsum('bqhd,bkhd->bhqk', qb, kc).astype(jnp.float32)
        p = jax.nn.softmax(jnp.concatenate([s_nb, s_cx], axis=-1), axis=-1).astype(v.dtype)
        return (jnp.einsum('bhqn,bqnhd->bqhd', p[..., :nk], vg)
                + jnp.einsum('bhqk,bkhd->bqhd', p[..., nk:], vc))

    qblk = q.reshape(B, nb, qb_len, H, d).swapaxes(0, 1)
    o = lax.map(block, (qblk, nbr.reshape(nb, qb_len, nk), rel.reshape(nb, qb_len, nk)))
    y_lat = o.swapaxes(0, 1).reshape(B, S, B_WIDTH) @ wo
    y_ctx = None
    if with_ctx_out:
        s = jnp.einsum('bqhd,bkhd->bhqk', qc * scale, kc).astype(jnp.float32)
        p = jax.nn.softmax(s, axis=-1).astype(vc.dtype)
        y_ctx = jnp.einsum('bhqk,bkhd->bqhd', p, vc).reshape(B, -1, B_WIDTH) @ wo
    return y_lat, y_ctx


def window_gqa(h_lat, h_ctx, wqkv, wo, sink, ang, with_ctx_out):
    B, S, _ = h_lat.shape
    H, KV, G, d = C_HEADS, C_KV_HEADS, C_GROUP, C_HEAD_DIM
    QB = Q_BLOCK
    C = h_ctx.shape[1]
    scale = d ** -0.5

    def project(h):
        n = h.shape[1]
        q, k, v = jnp.split(h @ wqkv, [H * d, H * d + KV * d], axis=-1)
        return q.reshape(B, n, H, d), k.reshape(B, n, KV, d), v.reshape(B, n, KV, d)

    q, k, v = project(h_lat)
    qc, kc, vc = project(h_ctx)
    q = rope_2d(q, *ang) * scale
    k = rope_2d(k, *ang)
    sink_f = sink.astype(jnp.float32).reshape(KV, G, 1, 1)

    nb = S // QB
    pad = ((0, 0), (QB, QB), (0, 0), (0, 0))
    kb = jnp.pad(k, pad).reshape(B, nb + 2, QB, KV, d)
    vb = jnp.pad(v, pad).reshape(B, nb + 2, QB, KV, d)
    k_band = jnp.concatenate([kb[:, :-2], kb[:, 1:-1], kb[:, 2:]], axis=2).swapaxes(0, 1)
    v_band = jnp.concatenate([vb[:, :-2], vb[:, 1:-1], vb[:, 2:]], axis=2).swapaxes(0, 1)
    qblk = q.reshape(B, nb, QB, KV, G, d).swapaxes(0, 1)
    nband = 3 * QB

    def block(args):
        qb, kk, vv, i = args
        qpos = i * QB + jnp.arange(QB)
        kpos = (i - 1) * QB + jnp.arange(nband)
        valid = (jnp.abs(qpos[:, None] - kpos[None, :]) <= C_WINDOW) & (kpos >= 0)[None, :] & (kpos < S)[None, :]
        s_band = jnp.einsum('bqkgd,bnkd->bkgqn', qb, kk).astype(jnp.float32)
        s_band = jnp.where(valid, s_band, -jnp.inf)
        s_cx = jnp.einsum('bqkgd,bnkd->bkgqn', qb, kc).astype(jnp.float32)
        snk = jnp.broadcast_to(sink_f, (B, KV, G, QB, 1))
        p = jax.nn.softmax(jnp.concatenate([s_band, s_cx, snk], axis=-1), axis=-1).astype(v.dtype)
        return (jnp.einsum('bkgqn,bnkd->bqkgd', p[..., :nband], vv)
                + jnp.einsum('bkgqn,bnkd->bqkgd', p[..., nband:nband + C], vc))

    o = lax.map(block, (qblk, k_band, v_band, jnp.arange(nb)))
    y_lat = o.swapaxes(0, 1).reshape(B, S, H * d) @ wo
    y_ctx = None
    if with_ctx_out:
        s = jnp.einsum('bqkgd,bnkd->bkgqn', qc.reshape(B, C, KV, G, d) * scale, kc).astype(jnp.float32)
        snk = jnp.broadcast_to(sink_f, (B, KV, G, C, 1))
        p = jax.nn.softmax(jnp.concatenate([s, snk], axis=-1), axis=-1).astype(vc.dtype)
        y_ctx = jnp.einsum('bkgqn,bnkd->bqkgd', p[..., :C], vc).reshape(B, C, H * d) @ wo
    return y_lat, y_ctx


def swiglu(h, w13, w2):
    gate, up = jnp.split(h @ w13, 2, axis=-1)
    return (jax.nn.silu(gate) * up) @ w2


def setup_inputs(seed: int = 0) -> dict:
    key = jax.random.key(seed)
    ks = jax.random.split(key, 19)
    D = D_MODEL

    def nrm(k, shape):
        return jax.random.normal(k, shape, jnp.float32)

    def w(k, shape, fan_in):
        return nrm(k, shape) * fan_in ** -0.5

    return {
        "x": nrm(ks[0], (BATCH, SEQ, D)),
        "c": nrm(ks[1], (BATCH, D)),
        "ctx": nrm(ks[2], (BATCH, CTX_LEN, D)),
        "c_ctx": nrm(ks[3], (D,)),
        "ada_w": w(ks[4], (DEPTH, D, 6 * D), D),
        "ada_b": 0.02 * nrm(ks[5], (DEPTH, 6 * D)),
        "norm_g": 1.0 + 0.1 * nrm(ks[6], (DEPTH, 4, D)),
        "ffn_w13": w(ks[7], (DEPTH, D, 2 * FFN_HIDDEN), D),
        "ffn_w2": w(ks[8], (DEPTH, FFN_HIDDEN, D), FFN_HIDDEN),
        "a_wqkv": w(ks[9], (N_A, D, 3 * A_WIDTH), D),
        "a_wo": w(ks[10], (N_A, A_WIDTH, D), A_WIDTH),
        "a_lambda": 0.1 * nrm(ks[11], (N_A, 4, A_HEAD_DIM)),
        "a_subln": 1.0 + 0.1 * nrm(ks[12], (N_A, 2 * A_HEAD_DIM)),
        "b_wqkv": w(ks[13], (N_B, D, 3 * B_WIDTH), D),
        "b_wo": w(ks[14], (N_B, B_WIDTH, D), B_WIDTH),
        "b_rpb": 0.5 * nrm(ks[15], (N_B, B_HEADS, 2 * NA_KH - 1, 2 * NA_KW - 1)),
        "c_wqkv": w(ks[16], (N_C, D, C_QKV), D),
        "c_wo": w(ks[17], (N_C, C_HEADS * C_HEAD_DIM, D), C_HEADS * C_HEAD_DIM),
        "c_sink": nrm(ks[18], (N_C, C_HEADS)),
    }


def reference(x, c, ctx, c_ctx, ada_w, ada_b, norm_g, ffn_w13, ffn_w2,
              a_wqkv, a_wo, a_lambda, a_subln, b_wqkv, b_wo, b_rpb, c_wqkv, c_wo, c_sink):
    S = x.shape[1]
    ang = grid_angles(S, A_HEAD_DIM)
    s_lat = jax.nn.silu(c)
    s_ctx = jax.nn.silu(c_ctx)
    for i in range(DEPTH):
        last = i == DEPTH - 1
        mixer, j = i % N_MIXERS, i // N_MIXERS
        g = norm_g[i]
        mod = (s_lat @ ada_w[i] + ada_b[i])[:, None, :]
        sh1, sc1, gt1, sh2, sc2, gt2 = jnp.split(mod, 6, axis=-1)
        modc = s_ctx @ ada_w[i] + ada_b[i]
        csh1, csc1, cgt1, csh2, csc2, cgt2 = jnp.split(modc, 6, axis=-1)

        h = rmsnorm(x, g[0]) * (1.0 + sc1) + sh1
        hc = rmsnorm(ctx, g[0]) * (1.0 + csc1) + csh1
        if mixer == 0:
            lam_init = 0.8 - 0.6 * math.exp(-0.3 * i)
            y, yc = diff_attention(h, hc, a_wqkv[j], a_wo[j], a_lambda[j], a_subln[j], lam_init, ang, not last)
        elif mixer == 1:
            y, yc = neighbourhood_attention(h, hc, b_wqkv[j], b_wo[j], b_rpb[j], not last)
        else:
            y, yc = window_gqa(h, hc, c_wqkv[j], c_wo[j], c_sink[j], ang, not last)

        x = x + gt1 * rmsnorm(y, g[1])
        h = rmsnorm(x, g[2]) * (1.0 + sc2) + sh2
        x = x + gt2 * rmsnorm(swiglu(h, ffn_w13[i], ffn_w2[i]), g[3])
        if not last:
            ctx = ctx + cgt1 * rmsnorm(yc, g[1])
            hc = rmsnorm(ctx, g[2]) * (1.0 + csc2) + csh2
            ctx = ctx + cgt2 * rmsnorm(swiglu(hc, ffn_w13[i], ffn_w2[i]), g[3])
    return x
```

```python
import functools
import math

import numpy as np
import jax
import jax.numpy as jnp
from jax import lax
from jax.experimental import pallas as pl
from jax.experimental.pallas import tpu as pltpu

F32 = jnp.float32
BF16 = jnp.bfloat16

D_MODEL = 1024
GRID_W = 64
CTX_LEN = 256
NORM_EPS = 1e-6
ROPE_THETA = 10000.0
HEAD_DIM = 64
NA_KH = 8
NA_KW = 16
NA_KEY_ROWS = 10
WINDOW = 128
Q_BLOCK = 128
LANES = 128
QUAD = 4 * HEAD_DIM
MASKED = -1e30
VMEM_LIMIT_BYTES = 56 * 1024 * 1024
_NT = (((1,), (1,)), ((), ()))


def _params(*sem):
    return pltpu.CompilerParams(dimension_semantics=sem, vmem_limit_bytes=VMEM_LIMIT_BYTES)


def _row_tile(n_rows):
    for tm in (640, 512, 256, 128):
        if n_rows % tm == 0:
            return tm
    raise ValueError(f"no row tile divides {n_rows}")


def _rms(x, g):
    return x * lax.rsqrt(jnp.mean(x * x, axis=-1, keepdims=True) + NORM_EPS) * g


def _is_ctx(tm, n_lat):
    row = pl.program_id(0) * tm + lax.broadcasted_iota(jnp.int32, (tm, 1), 0)
    return row >= n_lat


def _mod(mod_ref, idx, is_ctx):
    return jnp.where(is_ctx, mod_ref[1, idx:idx + 1, :], mod_ref[0, idx:idx + 1, :])


def _mods_kernel(c_ref, w_ref, b_ref, o_ref):
    s = c_ref[...]
    s = s * jax.nn.sigmoid(s)
    w = w_ref[0]
    b = b_ref[0]
    lat = jnp.sum(w * s[:, 0:1], axis=0, keepdims=True) + b
    cx = jnp.sum(w * s[:, 1:2], axis=0, keepdims=True) + b
    o_ref[0] = jnp.concatenate([lat, cx], axis=0)


def _mods(c, c_ctx, ada_w, ada_b):
    depth, d, n = ada_w.shape
    tn = 512
    c2 = jnp.stack([c[0], c_ctx], axis=1)
    return pl.pallas_call(
        _mods_kernel,
        grid=(depth, n // tn),
        in_specs=[
            pl.BlockSpec((d, 2), lambda l, j: (0, 0)),
            pl.BlockSpec((1, d, tn), lambda l, j: (l, 0, j)),
            pl.BlockSpec((1, 1, tn), lambda l, j: (l, 0, j)),
        ],
        out_specs=pl.BlockSpec((1, 2, tn), lambda l, j: (l, 0, j)),
        out_shape=jax.ShapeDtypeStruct((depth, 2, n), F32),
        compiler_params=_params("arbitrary", "arbitrary"),
        name="ada_mods",
    )(c2, ada_w, ada_b.reshape(depth, 1, n))


def _qkv_kernel(x_ref, mod_ref, g_ref, w_ref, cos_ref, sin_ref, o_ref, *, tm, n_lat, rope_cols, col_chunk):
    is_ctx = _is_ctx(tm, n_lat)
    h = _rms(x_ref[...], g_ref[...]) * (1.0 + _mod(mod_ref, 1, is_ctx)) + _mod(mod_ref, 0, is_ctx)
    h = h.astype(BF16)
    lane = lax.broadcasted_iota(jnp.int32, (1, LANES), 1)
    first_half = (lane % 32) < 16
    cos = cos_ref[...]
    sin = sin_ref[...]
    ncols = o_ref.shape[1]
    for c0 in range(0, ncols, col_chunk):
        acc = jnp.dot(h, w_ref[:, c0:c0 + col_chunk], preferred_element_type=F32)
        for s0 in range(0, col_chunk, LANES):
            a = acc[:, s0:s0 + LANES]
            if c0 + s0 < rope_cols:
                partner = jnp.where(first_half, pltpu.roll(a, LANES - 16, 1), pltpu.roll(a, 16, 1))
                a = a * cos + partner * sin
            o_ref[:, c0 + s0:c0 + s0 + LANES] = a.astype(BF16)


def _qkv(xa, mod, g, w, cos, sin, *, n_lat, rope_cols):
    n_rows, d = xa.shape
    ncols = w.shape[1]
    tm = _row_tile(n_rows)
    kern = functools.partial(_qkv_kernel, tm=tm, n_lat=n_lat, rope_cols=rope_cols, col_chunk=512)
    return pl.pallas_call(
        kern,
        grid=(n_rows // tm,),
        in_specs=[
            pl.BlockSpec((tm, d), lambda i: (i, 0)),
            pl.BlockSpec((2, 6, d), lambda i: (0, 0, 0)),
            pl.BlockSpec((1, d), lambda i: (0, 0)),
            pl.BlockSpec((d, ncols), lambda i: (0, 0)),
            pl.BlockSpec((tm, LANES), lambda i: (i, 0)),
            pl.BlockSpec((tm, LANES), lambda i: (i, 0)),
        ],
        out_specs=pl.BlockSpec((tm, ncols), lambda i: (i, 0)),
        out_shape=jax.ShapeDtypeStruct((n_rows, ncols), BF16),
        compiler_params=_params("arbitrary"),
        name="norm_qkv",
    )(xa, mod, g.reshape(1, d), w, cos, sin)


def _attn_out_kernel(o_ref, x_ref, mod_ref, g_ref, w_ref, out_ref, *, tm, n_lat):
    is_ctx = _is_ctx(tm, n_lat)
    y = jnp.dot(o_ref[...], w_ref[...], preferred_element_type=F32)
    out_ref[...] = x_ref[...] + _mod(mod_ref, 2, is_ctx) * _rms(y, g_ref[...])


def _attn_out(o, xa, mod, g, w, *, n_lat):
    n_rows, d = xa.shape
    tm = _row_tile(n_rows)
    kern = functools.partial(_attn_out_kernel, tm=tm, n_lat=n_lat)
    return pl.pallas_call(
        kern,
        grid=(n_rows // tm,),
        in_specs=[
            pl.BlockSpec((tm, o.shape[1]), lambda i: (i, 0)),
            pl.BlockSpec((tm, d), lambda i: (i, 0)),
            pl.BlockSpec((2, 6, d), lambda i: (0, 0, 0)),
            pl.BlockSpec((1, d), lambda i: (0, 0)),
            pl.BlockSpec(w.shape, lambda i: (0, 0)),
        ],
        out_specs=pl.BlockSpec((tm, d), lambda i: (i, 0)),
        out_shape=jax.ShapeDtypeStruct((n_rows, d), F32),
        compiler_params=_params("arbitrary"),
        name="attn_out",
    )(o, xa, mod, g.reshape(1, d), w)


def _ffn_kernel(x_ref, mod_ref, g2_ref, g3_ref, w13_ref, w2_ref, out_ref, *, tm, n_lat, hidden, n_chunks):
    is_ctx = _is_ctx(tm, n_lat)
    x = x_ref[...]
    h = _rms(x, g2_ref[...]) * (1.0 + _mod(mod_ref, 4, is_ctx)) + _mod(mod_ref, 3, is_ctx)
    h = h.astype(BF16)
    hc = hidden // n_chunks
    f = None
    for c in range(n_chunks):
        gate = jnp.dot(h, w13_ref[:, c * hc:(c + 1) * hc], preferred_element_type=F32)
        up = jnp.dot(h, w13_ref[:, hidden + c * hc:hidden + (c + 1) * hc], preferred_element_type=F32)
        act = (gate * jax.nn.sigmoid(gate) * up).astype(BF16)
        part = jnp.dot(act, w2_ref[c * hc:(c + 1) * hc, :], preferred_element_type=F32)
        f = part if f is None else f + part
    out_ref[...] = x + _mod(mod_ref, 5, is_ctx) * _rms(f, g3_ref[...])


def _ffn(xa, mod, g2, g3, w13, w2, *, n_lat):
    n_rows, d = xa.shape
    hidden = w2.shape[0]
    tm = _row_tile(n_rows)
    n_chunks = 2 if (hidden // 2) % LANES == 0 else 1
    kern = functools.partial(_ffn_kernel, tm=tm, n_lat=n_lat, hidden=hidden, n_chunks=n_chunks)
    return pl.pallas_call(
        kern,
        grid=(n_rows // tm,),
        in_specs=[
            pl.BlockSpec((tm, d), lambda i: (i, 0)),
            pl.BlockSpec((2, 6, d), lambda i: (0, 0, 0)),
            pl.BlockSpec((1, d), lambda i: (0, 0)),
            pl.BlockSpec((1, d), lambda i: (0, 0)),
            pl.BlockSpec(w13.shape, lambda i: (0, 0), pipeline_mode=pl.Buffered(1)),
            pl.BlockSpec(w2.shape, lambda i: (0, 0), pipeline_mode=pl.Buffered(1)),
        ],
        out_specs=pl.BlockSpec((tm, d), lambda i: (i, 0)),
        out_shape=jax.ShapeDtypeStruct((n_rows, d), F32),
        compiler_params=_params("arbitrary"),
        name="ffn",
    )(xa, mod, g2.reshape(1, d), g3.reshape(1, d), w13, w2)


def _diff_kernel(lam_ref, subln_ref, q_ref, k_ref, v_ref, o_ref, m_sc, l_sc, acc_sc, *, tq, tk, n_lat, lam_init):
    i = pl.program_id(1)
    lane = lax.broadcasted_iota(jnp.int32, (1, LANES), 1)
    lam = lam_ref[...]
    lam_full = (jnp.exp(jnp.sum(lam[0:1] * lam[1:2], axis=1, keepdims=True))
                - jnp.exp(jnp.sum(lam[2:3] * lam[3:4], axis=1, keepdims=True)) + lam_init)
    for hh in range(2):
        cs = slice(hh * LANES, (hh + 1) * LANES)
        q = q_ref[:, cs]
        zero = jnp.zeros_like(q)
        qq = jnp.concatenate([jnp.where(lane < HEAD_DIM, q, zero), jnp.where(lane >= HEAD_DIM, q, zero)], axis=0)
        m_sc[...] = jnp.full(m_sc.shape, -jnp.inf, F32)
        l_sc[...] = jnp.zeros(l_sc.shape, F32)
        acc_sc[...] = jnp.zeros(acc_sc.shape, F32)

        def chunk(row0, size, qq=qq, cs=cs):
            k = k_ref[pl.ds(row0, size), cs]
            v = v_ref[pl.ds(row0, size), :]
            s = lax.dot_general(qq, k, _NT, preferred_element_type=F32)
            m_prev = m_sc[...]
            m_next = jnp.maximum(m_prev, jnp.max(s, axis=1)[:, None])
            p = jnp.exp(s - jnp.tile(m_next, (1, size // LANES)))
            alpha = jnp.exp(m_prev - m_next)
            l_sc[...] = alpha * l_sc[...] + jnp.sum(p, axis=1)[:, None]
            pv = jnp.dot(p.astype(BF16), v, preferred_element_type=F32)
            acc_sc[...] = alpha * acc_sc[...] + pv[:, cs]
            m_sc[...] = m_next

        chunk(n_lat, CTX_LEN)

        @pl.when(i < n_lat // tq)
        def _(chunk=chunk):
            def body(c, carry):
                chunk(pl.multiple_of(c * tk, tk), tk)
                return carry
            lax.fori_loop(0, n_lat // tk, body, 0)

        o2 = acc_sc[...] / l_sc[...]
        o = o2[:tq] - lam_full * o2[tq:]
        o = _rms(o, subln_ref[...]) * (1.0 - lam_init)
        o_ref[:, cs] = o.astype(BF16)


def _diff_attention(qkv, lam, subln, *, n_lat, lam_init):
    n_rows = qkv.shape[0]
    width = qkv.shape[1] // 3
    n_pairs = width // QUAD
    tq = 256
    tk = 1024 if n_lat % 1024 == 0 else 256
    kern = functools.partial(_diff_kernel, tq=tq, tk=tk, n_lat=n_lat, lam_init=lam_init)
    kv_spec = lambda off: pl.BlockSpec((n_rows, QUAD), lambda p, i: (0, off + p), pipeline_mode=pl.Buffered(1))
    return pl.pallas_call(
        kern,
        grid=(n_pairs, n_rows // tq),
        in_specs=[
            pl.BlockSpec(lam.shape, lambda p, i: (0, 0)),
            pl.BlockSpec((1, 2 * HEAD_DIM), lambda p, i: (0, 0)),
            pl.BlockSpec((tq, QUAD), lambda p, i: (i, p)),
            kv_spec(n_pairs),
            kv_spec(2 * n_pairs),
        ],
        out_specs=pl.BlockSpec((tq, QUAD), lambda p, i: (i, p)),
        out_shape=jax.ShapeDtypeStruct((n_rows, width), BF16),
        scratch_shapes=[pltpu.VMEM((2 * tq, LANES), F32)] * 3,
        compiler_params=_params("arbitrary", "arbitrary"),
        name="diff_attn",
    )(lam, subln.reshape(1, -1), qkv, qkv, qkv)


def _quad_kernel(*refs, mode, n_lat, nb):
    if mode == "na":
        _, q_ref, k_ref, v_ref, bias_ref, o_ref = refs
    else:
        q_ref, k_ref, v_ref, sink_ref, o_ref = refs
    i = pl.program_id(1)
    lane = lax.broadcasted_iota(jnp.int32, (1, QUAD), 1)
    q = q_ref[...]
    zero = jnp.zeros_like(q)
    qq = jnp.concatenate([jnp.where((lane // HEAD_DIM) == h, q, zero) for h in range(4)], axis=0)
    if mode == "na":
        base = jnp.clip(2 * i - NA_KH // 2, 0, n_lat // GRID_W - NA_KEY_ROWS)
        start = pl.multiple_of(base * GRID_W, GRID_W)
    else:
        start = pl.multiple_of(jnp.clip((i - 1) * Q_BLOCK, 0, n_lat - nb), Q_BLOCK)
    kb = k_ref[pl.ds(start, nb), :]
    vb = v_ref[pl.ds(start, nb), :]
    kc = k_ref[n_lat:n_lat + CTX_LEN, :]
    vc = v_ref[n_lat:n_lat + CTX_LEN, :]
    sb = lax.dot_general(qq, kb, _NT, preferred_element_type=F32)
    sc = lax.dot_general(qq, kc, _NT, preferred_element_type=F32)
    if mode == "na":
        sb = sb + bias_ref[0, 0]
    else:
        q0 = i * Q_BLOCK + jnp.where(i >= n_lat // Q_BLOCK, 1 << 24, 0)
        qpos = q0 + lax.broadcasted_iota(jnp.int32, (4 * Q_BLOCK, nb), 0) % Q_BLOCK
        kpos = start + lax.broadcasted_iota(jnp.int32, (4 * Q_BLOCK, nb), 1)
        sb = jnp.where(jnp.abs(qpos - kpos) <= WINDOW, sb, MASKED)
    m = jnp.maximum(jnp.max(sb, axis=1, keepdims=True), jnp.max(sc, axis=1, keepdims=True))
    if mode == "win":
        sink = sink_ref[0]
        m = jnp.maximum(m, sink)
    pb = jnp.exp(sb - m)
    pc = jnp.exp(sc - m)
    l = jnp.sum(pb, axis=1, keepdims=True) + jnp.sum(pc, axis=1, keepdims=True)
    if mode == "win":
        l = l + jnp.exp(sink - m)
    pv = (jnp.dot(pb.astype(BF16), vb, preferred_element_type=F32)
          + jnp.dot(pc.astype(BF16), vc, preferred_element_type=F32)) / l
    o = jnp.zeros((Q_BLOCK, QUAD), F32)
    for h in range(4):
        o = jnp.where((lane // HEAD_DIM) == h, pv[h * Q_BLOCK:(h + 1) * Q_BLOCK], o)
    o_ref[...] = o.astype(BF16)


def _na_index_tables(n_lat):
    rows = n_lat // GRID_W
    kh = min(NA_KH, rows)
    qr = np.repeat(np.arange(2), GRID_W)
    qc = np.tile(np.arange(GRID_W), 2)
    kj = np.repeat(np.arange(NA_KEY_ROWS), GRID_W)
    kc = np.tile(np.arange(GRID_W), NA_KEY_ROWS)
    tables, index_of, types = [], {}, []
    for b in range(rows // 2):
        base = int(np.clip(2 * b - NA_KH // 2, 0, rows - NA_KEY_ROWS))
        r = 2 * b + qr
        kr = base + kj
        r0 = np.clip(r - kh // 2, 0, rows - kh)
        c0 = np.clip(qc - NA_KW // 2, 0, GRID_W - NA_KW)
        valid = ((kr[None] >= r0[:, None]) & (kr[None] < r0[:, None] + kh)
                 & (kc[None] >= c0[:, None]) & (kc[None] < c0[:, None] + NA_KW))
        assert (valid.sum(1) == kh * NA_KW).all()
        drow = kr[None] - r[:, None] + (NA_KH - 1)
        dcol = kc[None] - qc[:, None] + (NA_KW - 1)
        idx = np.where(valid, drow * (2 * NA_KW - 1) + dcol, -1).astype(np.int32)
        key = idx.tobytes()
        if key not in index_of:
            index_of[key] = len(tables)
            tables.append(idx)
        types.append(index_of[key])
    tables.append(np.full_like(tables[0], -1))
    types += [len(tables) - 1] * (CTX_LEN // Q_BLOCK)
    return np.stack(tables), np.asarray(types, np.int32)


def _na_attention(qkv, rpb, *, n_lat):
    n_rows = qkv.shape[0]
    width = qkv.shape[1] // 3
    n_quads = width // QUAD
    nb = NA_KEY_ROWS * GRID_W
    idx, types = _na_index_tables(n_lat)
    n_types = idx.shape[0]
    heads = rpb.shape[0]
    rpb_flat = rpb.reshape(heads, -1).astype(F32)
    bias = jnp.where(idx[None] >= 0, jnp.take(rpb_flat, np.maximum(idx, 0), axis=1), MASKED)
    bias = bias.reshape(n_quads, 4, n_types, Q_BLOCK, nb).transpose(2, 0, 1, 3, 4).reshape(n_types, n_quads, 4 * Q_BLOCK, nb)
    kern = functools.partial(_quad_kernel, mode="na", n_lat=n_lat, nb=nb)
    kv_spec = lambda off: pl.BlockSpec((n_rows, QUAD), lambda j, i, t: (0, off + j), pipeline_mode=pl.Buffered(1))
    return pl.pallas_call(
        kern,
        grid_spec=pltpu.PrefetchScalarGridSpec(
            num_scalar_prefetch=1,
            grid=(n_quads, n_rows // Q_BLOCK),
            in_specs=[
                pl.BlockSpec((Q_BLOCK, QUAD), lambda j, i, t: (i, j)),
                kv_spec(n_quads),
                kv_spec(2 * n_quads),
                pl.BlockSpec((1, 1, 4 * Q_BLOCK, nb), lambda j, i, t: (t[i], j, 0, 0)),
            ],
            out_specs=pl.BlockSpec((Q_BLOCK, QUAD), lambda j, i, t: (i, j)),
        ),
        out_shape=jax.ShapeDtypeStruct((n_rows, width), BF16),
        compiler_params=_params("arbitrary", "arbitrary"),
        name="na_attn",
    )(jnp.asarray(types), qkv, qkv, qkv, bias)


def _window_attention(qkv, sink, *, n_lat):
    n_rows = qkv.shape[0]
    n_groups = sink.shape[0] // 4
    width = n_groups * QUAD
    nb = 3 * Q_BLOCK
    sink_col = jnp.repeat(sink.astype(F32).reshape(4, n_groups).T, Q_BLOCK, axis=1).reshape(n_groups, 4 * Q_BLOCK, 1)
    kern = functools.partial(_quad_kernel, mode="win", n_lat=n_lat, nb=nb)
    kv_spec = lambda off: pl.BlockSpec((n_rows, QUAD), lambda g, i: (0, off), pipeline_mode=pl.Buffered(1))
    return pl.pallas_call(
        kern,
        grid=(n_groups, n_rows // Q_BLOCK),
        in_specs=[
            pl.BlockSpec((Q_BLOCK, QUAD), lambda g, i: (i, g)),
            kv_spec(n_groups),
            kv_spec(n_groups + 1),
            pl.BlockSpec((1, 4 * Q_BLOCK, 1), lambda g, i: (g, 0, 0)),
        ],
        out_specs=pl.BlockSpec((Q_BLOCK, QUAD), lambda g, i: (i, g)),
        out_shape=jax.ShapeDtypeStruct((n_rows, width), BF16),
        compiler_params=_params("arbitrary", "arbitrary"),
        name="window_attn",
    )(qkv, qkv, qkv, sink_col)


def _rope_tables(n_lat):
    half = HEAD_DIM // 2
    freqs = ROPE_THETA ** (-jnp.arange(0, half, 2, dtype=F32) / half)
    pos = jnp.arange(n_lat)
    ang_r = (pos // GRID_W).astype(F32)[:, None] * freqs
    ang_c = (pos % GRID_W).astype(F32)[:, None] * freqs
    cos = jnp.concatenate([jnp.cos(ang_r)] * 2 + [jnp.cos(ang_c)] * 2, axis=-1)
    sin = jnp.concatenate([-jnp.sin(ang_r), jnp.sin(ang_r), -jnp.sin(ang_c), jnp.sin(ang_c)], axis=-1)
    cos = jnp.concatenate([jnp.tile(cos, (1, 2)), jnp.ones((CTX_LEN, LANES), F32)], axis=0)
    sin = jnp.concatenate([jnp.tile(sin, (1, 2)), jnp.zeros((CTX_LEN, LANES), F32)], axis=0)
    return cos, sin


def _scale_q(w, q_cols):
    scale = jnp.concatenate([jnp.full((q_cols,), HEAD_DIM ** -0.5, F32), jnp.ones((w.shape[1] - q_cols,), F32)])
    return w * scale


def kernel(x, c, ctx, c_ctx, ada_w, ada_b, norm_g, ffn_w13, ffn_w2, a_wqkv, a_wo, a_lambda, a_subln,
           b_wqkv, b_wo, b_rpb, c_wqkv, c_wo, c_sink):
    depth = ada_w.shape[0]
    n_lat = x.shape[1]
    d = x.shape[2]
    xa = jnp.concatenate([x[0], ctx[0]], axis=0)
    mods = _mods(c, c_ctx, ada_w, ada_b).reshape(depth, 2, 6, d)
    cos, sin = _rope_tables(n_lat)

    n_heads_c = c_sink.shape[1]
    kv_c = (c_wqkv.shape[2] - n_heads_c * HEAD_DIM) // (2 * HEAD_DIM)
    grp_c = n_heads_c // kv_c
    perm = (np.arange(kv_c)[None, :, None] * grp_c + np.arange(grp_c)[:, None, None]) * HEAD_DIM + np.arange(HEAD_DIM)[None, None, :]
    perm = perm.reshape(-1)

    for i in range(depth):
        mixer, j = i % 3, i // 3
        g = norm_g[i]
        if mixer == 0:
            width = a_wo.shape[1]
            w = _scale_q(a_wqkv[j], width).astype(BF16)
            qkv = _qkv(xa, mods[i], g[0], w, cos, sin, n_lat=n_lat, rope_cols=2 * width)
            lam_init = 0.8 - 0.6 * math.exp(-0.3 * i)
            o = _diff_attention(qkv, a_lambda[j], a_subln[j], n_lat=n_lat, lam_init=lam_init)
            wo = a_wo[j]
        elif mixer == 1:
            width = b_wo.shape[1]
            w = _scale_q(b_wqkv[j], width).astype(BF16)
            qkv = _qkv(xa, mods[i], g[0], w, cos, sin, n_lat=n_lat, rope_cols=0)
            o = _na_attention(qkv, b_rpb[j], n_lat=n_lat)
            wo = b_wo[j]
        else:
            width = n_heads_c * HEAD_DIM
            wq = c_wqkv[j]
            w = jnp.concatenate([wq[:, :width][:, perm], wq[:, width:]], axis=1)
            w = _scale_q(w, width).astype(BF16)
            qkv = _qkv(xa, mods[i], g[0], w, cos, sin, n_lat=n_lat, rope_cols=width + kv_c * HEAD_DIM)
            o = _window_attention(qkv, c_sink[j], n_lat=n_lat)
            wo = c_wo[j][perm, :]
        xa = _attn_out(o, xa, mods[i], g[1], wo.astype(BF16), n_lat=n_lat)
        xa = _ffn(xa, mods[i], g[2], g[3], ffn_w13[i].astype(BF16), ffn_w2[i].astype(BF16), n_lat=n_lat)
    return xa[:n_lat][None]
```

```python
import functools
import math

import numpy as np
import jax
import jax.numpy as jnp
from jax import lax
from jax.experimental import pallas as pl
from jax.experimental.pallas import tpu as pltpu

F32 = jnp.float32
BF16 = jnp.bfloat16

GRID_W = 64
CTX_LEN = 256
NORM_EPS = 1e-6
ROPE_THETA = 10000.0
HEAD_DIM = 64
NA_KH = 8
NA_KW = 16
NA_KEY_ROWS = 10
WINDOW = 128
Q_BLOCK = 128
LANES = 128
QUAD = 4 * HEAD_DIM
MASKED = -1e30
LOG2E = math.log2(math.e)
VMEM_LIMIT_BYTES = 56 * 1024 * 1024
_NT = (((1,), (1,)), ((), ()))


def _params(*sem, flags=None):
    return pltpu.CompilerParams(dimension_semantics=sem, vmem_limit_bytes=VMEM_LIMIT_BYTES, flags=flags)


def _row_tile(n_rows):
    for tm in (640, 512, 256, 128):
        if n_rows % tm == 0:
            return tm
    raise ValueError(f"no row tile divides {n_rows}")


def _rms(x, g):
    return x * lax.rsqrt(jnp.mean(x * x, axis=-1, keepdims=True) + NORM_EPS) * g


def _is_ctx(tm, n_lat):
    row = pl.program_id(0) * tm + lax.broadcasted_iota(jnp.int32, (tm, 1), 0)
    return row >= n_lat


def _mod(mod_ref, idx, is_ctx):
    return jnp.where(is_ctx, mod_ref[1, idx:idx + 1, :], mod_ref[0, idx:idx + 1, :])


def _mods_kernel(c_ref, w_ref, b_ref, o_ref):
    s = c_ref[...]
    s = s * jax.nn.sigmoid(s)
    w = w_ref[0]
    b = b_ref[0]
    lat = jnp.sum(w * s[:, 0:1], axis=0, keepdims=True) + b
    cx = jnp.sum(w * s[:, 1:2], axis=0, keepdims=True) + b
    o_ref[0] = jnp.concatenate([lat, cx], axis=0)


def _mods(c, c_ctx, ada_w, ada_b):
    depth, d, n = ada_w.shape
    tn = 512
    c2 = jnp.stack([c[0], c_ctx], axis=1)
    return pl.pallas_call(
        _mods_kernel,
        grid=(depth, n // tn),
        in_specs=[
            pl.BlockSpec((d, 2), lambda l, j: (0, 0)),
            pl.BlockSpec((1, d, tn), lambda l, j: (l, 0, j)),
            pl.BlockSpec((1, 1, tn), lambda l, j: (l, 0, j)),
        ],
        out_specs=pl.BlockSpec((1, 2, tn), lambda l, j: (l, 0, j)),
        out_shape=jax.ShapeDtypeStruct((depth, 2, n), F32),
        compiler_params=_params("arbitrary", "arbitrary"),
        name="ada_mods",
    )(c2, ada_w, ada_b.reshape(depth, 1, n))


def _qkv_kernel(x_ref, mod_ref, g_ref, w_ref, cos_ref, sin_ref, o_ref, *, tm, n_lat, rope_cols, col_chunk, out_slabs,
                ones_slabs):
    is_ctx = _is_ctx(tm, n_lat)
    h = _rms(x_ref[...], g_ref[...]) * (1.0 + _mod(mod_ref, 1, is_ctx)) + _mod(mod_ref, 0, is_ctx)
    h = h.astype(BF16)
    lane = lax.broadcasted_iota(jnp.int32, (1, LANES), 1)
    first_half = (lane % 32) < 16
    cos = cos_ref[...]
    sin = sin_ref[...]
    for c0 in range(0, w_ref.shape[1], col_chunk):
        acc = jnp.dot(h, w_ref[:, c0:c0 + col_chunk], preferred_element_type=F32)
        for s0 in range(0, col_chunk, LANES):
            a = acc[:, s0:s0 + LANES]
            if c0 + s0 < rope_cols:
                partner = jnp.where(first_half, pltpu.roll(a, LANES - 16, 1), pltpu.roll(a, 16, 1))
                a = a * cos + partner * sin
            dst = out_slabs[(c0 + s0) // LANES] * LANES
            o_ref[:, dst:dst + LANES] = a.astype(BF16)
    for slab in ones_slabs:
        o_ref[:, slab * LANES:(slab + 1) * LANES] = jnp.ones((tm, LANES), BF16)


def _qkv(xa, mod, g, w, cos, sin, *, n_lat, rope_cols, out_slabs=None, ones_slabs=()):
    n_rows, d = xa.shape
    n_in = w.shape[1] // LANES
    out_slabs = tuple(range(n_in)) if out_slabs is None else out_slabs
    ncols = (n_in + len(ones_slabs)) * LANES
    tm = _row_tile(n_rows)
    kern = functools.partial(_qkv_kernel, tm=tm, n_lat=n_lat, rope_cols=rope_cols, col_chunk=512,
                             out_slabs=out_slabs, ones_slabs=ones_slabs)
    return pl.pallas_call(
        kern,
        grid=(n_rows // tm,),
        in_specs=[
            pl.BlockSpec((tm, d), lambda i: (i, 0)),
            pl.BlockSpec((2, 6, d), lambda i: (0, 0, 0)),
            pl.BlockSpec((1, d), lambda i: (0, 0)),
            pl.BlockSpec(w.shape, lambda i: (0, 0)),
            pl.BlockSpec((tm, LANES), lambda i: (i, 0)),
            pl.BlockSpec((tm, LANES), lambda i: (i, 0)),
        ],
        out_specs=pl.BlockSpec((tm, ncols), lambda i: (i, 0)),
        out_shape=jax.ShapeDtypeStruct((n_rows, ncols), BF16),
        compiler_params=_params("arbitrary"),
        name="norm_qkv",
    )(xa, mod, g.reshape(1, d), w, cos, sin)


def _attn_out_kernel(o_ref, x_ref, mod_ref, g_ref, w_ref, out_ref, *, tm, n_lat):
    is_ctx = _is_ctx(tm, n_lat)
    y = jnp.dot(o_ref[...], w_ref[...], preferred_element_type=F32)
    out_ref[...] = x_ref[...] + _mod(mod_ref, 2, is_ctx) * _rms(y, g_ref[...])


def _attn_out(o, xa, mod, g, w, *, n_lat):
    n_rows, d = xa.shape
    tm = _row_tile(n_rows)
    kern = functools.partial(_attn_out_kernel, tm=tm, n_lat=n_lat)
    return pl.pallas_call(
        kern,
        grid=(n_rows // tm,),
        in_specs=[
            pl.BlockSpec((tm, o.shape[1]), lambda i: (i, 0)),
            pl.BlockSpec((tm, d), lambda i: (i, 0)),
            pl.BlockSpec((2, 6, d), lambda i: (0, 0, 0)),
            pl.BlockSpec((1, d), lambda i: (0, 0)),
            pl.BlockSpec(w.shape, lambda i: (0, 0)),
        ],
        out_specs=pl.BlockSpec((tm, d), lambda i: (i, 0)),
        out_shape=jax.ShapeDtypeStruct((n_rows, d), F32),
        compiler_params=_params("arbitrary"),
        name="attn_out",
    )(o, xa, mod, g.reshape(1, d), w)


def _ffn_kernel(x_ref, mod_ref, g2_ref, g3_ref, w13_ref, w2_ref, out_ref, *, tm, n_lat, hidden, n_chunks):
    is_ctx = _is_ctx(tm, n_lat)
    x = x_ref[...]
    h = _rms(x, g2_ref[...]) * (1.0 + _mod(mod_ref, 4, is_ctx)) + _mod(mod_ref, 3, is_ctx)
    h = h.astype(BF16)
    hc = hidden // n_chunks
    f = None
    for c in range(n_chunks):
        gate = jnp.dot(h, w13_ref[:, c * hc:(c + 1) * hc], preferred_element_type=F32)
        up = jnp.dot(h, w13_ref[:, hidden + c * hc:hidden + (c + 1) * hc], preferred_element_type=F32)
        act = (gate * jax.nn.sigmoid(gate) * up).astype(BF16)
        part = jnp.dot(act, w2_ref[c * hc:(c + 1) * hc, :], preferred_element_type=F32)
        f = part if f is None else f + part
    out_ref[...] = x + _mod(mod_ref, 5, is_ctx) * _rms(f, g3_ref[...])


def _ffn(xa, mod, g2, g3, w13, w2, *, n_lat):
    n_rows, d = xa.shape
    hidden = w2.shape[0]
    tm = _row_tile(n_rows)
    n_chunks = 2 if (hidden // 2) % LANES == 0 else 1
    kern = functools.partial(_ffn_kernel, tm=tm, n_lat=n_lat, hidden=hidden, n_chunks=n_chunks)
    return pl.pallas_call(
        kern,
        grid=(n_rows // tm,),
        in_specs=[
            pl.BlockSpec((tm, d), lambda i: (i, 0)),
            pl.BlockSpec((2, 6, d), lambda i: (0, 0, 0)),
            pl.BlockSpec((1, d), lambda i: (0, 0)),
            pl.BlockSpec((1, d), lambda i: (0, 0)),
            pl.BlockSpec(w13.shape, lambda i: (0, 0), pipeline_mode=pl.Buffered(1)),
            pl.BlockSpec(w2.shape, lambda i: (0, 0), pipeline_mode=pl.Buffered(1)),
        ],
        out_specs=pl.BlockSpec((tm, d), lambda i: (i, 0)),
        out_shape=jax.ShapeDtypeStruct((n_rows, d), F32),
        compiler_params=_params("arbitrary"),
        name="ffn",
    )(xa, mod, g2.reshape(1, d), g3.reshape(1, d), w13, w2)


def _diff_kernel(lam_ref, subln_ref, q_ref, k_ref, v_ref, o_ref, s0_sc, s1_sc, p0_sc, p1_sc, a0_sc, a1_sc, m_sc, l_sc,
                 acc_sc, *, tq, tk, n_lat, lam_init):
    i = pl.program_id(1)
    n_chunks = n_lat // tk
    s_sc, p_sc, a_sc = (s0_sc, s1_sc), (p0_sc, p1_sc), (a0_sc, a1_sc)
    lane = lax.broadcasted_iota(jnp.int32, (1, LANES), 1)
    q = q_ref[...]
    zero = jnp.zeros_like(q)
    qq = jnp.concatenate([jnp.where(lane < HEAD_DIM, q, zero), jnp.where(lane >= HEAD_DIM, q, zero)], axis=0)
    m_sc[...] = jnp.full(m_sc.shape, -jnp.inf, F32)
    l_sc[...] = jnp.zeros(l_sc.shape, F32)
    acc_sc[...] = jnp.zeros(acc_sc.shape, F32)

    def scores(row0, size):
        return lax.dot_general(qq, k_ref[pl.ds(row0, size), :], _NT, preferred_element_type=F32)

    def softmax(s):
        m_prev = m_sc[...]
        m_next = jnp.maximum(m_prev, jnp.max(s, axis=1)[:, None])
        m_sc[...] = m_next
        p = jnp.exp2(s - jnp.tile(m_next, (1, s.shape[1] // LANES)))
        return p.astype(BF16), jnp.exp2(m_prev - m_next)

    def accumulate(p, alpha, row0, size):
        pv = jnp.dot(p, v_ref[pl.ds(row0, size), :], preferred_element_type=F32)
        acc_sc[...] = alpha * acc_sc[...] + pv[:, :LANES]
        l_sc[...] = alpha * l_sc[...] + pv[:, LANES:]

    p, alpha = softmax(scores(n_lat, CTX_LEN))
    accumulate(p, alpha, n_lat, CTX_LEN)

    @pl.when(i < n_lat // tq)
    def _():
        def qk(t, slot):
            s_sc[slot][...] = scores(pl.multiple_of(t * tk, tk), tk)

        def sm(slot):
            p, alpha = softmax(s_sc[slot][...])
            p_sc[slot][...] = p
            a_sc[slot][...] = alpha

        def pv(t, slot):
            accumulate(p_sc[slot][...], a_sc[slot][...], pl.multiple_of(t * tk, tk), tk)

        qk(0, 0)
        qk(1, 1)
        sm(0)

        def body(u, carry):
            t = 2 * u + 1
            qk(t + 1, 0); sm(1); pv(t - 1, 0)
            qk(t + 2, 1); sm(0); pv(t, 1)
            return carry

        lax.fori_loop(0, (n_chunks - 2) // 2, body, 0)
        sm(1)
        pv(n_chunks - 2, 0)
        pv(n_chunks - 1, 1)

    lam = lam_ref[...]
    lam_full = (jnp.exp(jnp.sum(lam[0:1] * lam[1:2], axis=1, keepdims=True))
                - jnp.exp(jnp.sum(lam[2:3] * lam[3:4], axis=1, keepdims=True)) + lam_init)
    o2 = acc_sc[...] / l_sc[...]
    o = o2[:tq] - lam_full * o2[tq:]
    o = _rms(o, subln_ref[...]) * (1.0 - lam_init)
    o_ref[...] = o.astype(BF16)


def _diff_attention(qkv, lam, subln, *, n_lat, n_heads, lam_init):
    n_rows = qkv.shape[0]
    tq = 256
    tk = 1024 if n_lat % 2048 == 0 else 256
    assert (n_lat // tk) % 2 == 0
    kern = functools.partial(_diff_kernel, tq=tq, tk=tk, n_lat=n_lat, lam_init=lam_init)
    return pl.pallas_call(
        kern,
        grid=(n_heads, n_rows // tq),
        in_specs=[
            pl.BlockSpec(lam.shape, lambda h, i: (0, 0)),
            pl.BlockSpec((1, LANES), lambda h, i: (0, 0)),
            pl.BlockSpec((tq, LANES), lambda h, i: (i, h)),
            pl.BlockSpec((n_rows, LANES), lambda h, i: (0, n_heads + h), pipeline_mode=pl.Buffered(1)),
            pl.BlockSpec((n_rows, 2 * LANES), lambda h, i: (0, n_heads + h), pipeline_mode=pl.Buffered(1)),
        ],
        out_specs=pl.BlockSpec((tq, LANES), lambda h, i: (i, h)),
        out_shape=jax.ShapeDtypeStruct((n_rows, n_heads * LANES), BF16),
        scratch_shapes=[
            pltpu.VMEM((2 * tq, tk), F32), pltpu.VMEM((2 * tq, tk), F32),
            pltpu.VMEM((2 * tq, tk), BF16), pltpu.VMEM((2 * tq, tk), BF16),
            pltpu.VMEM((2 * tq, LANES), F32), pltpu.VMEM((2 * tq, LANES), F32),
            pltpu.VMEM((2 * tq, LANES), F32),
            pltpu.VMEM((2 * tq, LANES), F32),
            pltpu.VMEM((2 * tq, LANES), F32),
        ],
        compiler_params=_params("arbitrary", "arbitrary"),
        name="diff_attn",
    )(lam, subln.reshape(1, -1), qkv, qkv, qkv)


def _quad_block(q, blk, k_ref, v_ref, bias, sink, *, mode, n_lat, nb):
    lane = lax.broadcasted_iota(jnp.int32, (1, QUAD), 1)
    zero = jnp.zeros_like(q)
    qq = jnp.concatenate([jnp.where((lane // HEAD_DIM) == h, q, zero) for h in range(4)], axis=0)
    if mode == "na":
        base = jnp.clip(2 * blk - NA_KH // 2, 0, n_lat // GRID_W - NA_KEY_ROWS)
        start = pl.multiple_of(base * GRID_W, GRID_W)
    else:
        start = pl.multiple_of(jnp.clip((blk - 1) * Q_BLOCK, 0, n_lat - nb), Q_BLOCK)
    kb = k_ref[pl.ds(start, nb), :]
    vb = v_ref[pl.ds(start, nb), :]
    kc = k_ref[n_lat:n_lat + CTX_LEN, :]
    vc = v_ref[n_lat:n_lat + CTX_LEN, :]
    sb = lax.dot_general(qq, kb, _NT, preferred_element_type=F32)
    sc = lax.dot_general(qq, kc, _NT, preferred_element_type=F32)
    if mode == "na":
        sb = sb + bias
    else:
        q0 = blk * Q_BLOCK + jnp.where(blk >= n_lat // Q_BLOCK, 1 << 24, 0)
        qpos = q0 + lax.broadcasted_iota(jnp.int32, (4 * Q_BLOCK, nb), 0) % Q_BLOCK
        kpos = start + lax.broadcasted_iota(jnp.int32, (4 * Q_BLOCK, nb), 1)
        sb = jnp.where(jnp.abs(qpos - kpos) <= WINDOW, sb, MASKED)
    m = jnp.maximum(jnp.max(sb, axis=1, keepdims=True), jnp.max(sc, axis=1, keepdims=True))
    if mode == "win":
        m = jnp.maximum(m, sink)
    pb = jnp.exp(sb - m)
    pc = jnp.exp(sc - m)
    l = jnp.sum(pb, axis=1, keepdims=True) + jnp.sum(pc, axis=1, keepdims=True)
    if mode == "win":
        l = l + jnp.exp(sink - m)
    pv = (jnp.dot(pb.astype(BF16), vb, preferred_element_type=F32)
          + jnp.dot(pc.astype(BF16), vc, preferred_element_type=F32)) / l
    o = jnp.zeros((Q_BLOCK, QUAD), F32)
    for h in range(4):
        o = jnp.where((lane // HEAD_DIM) == h, pv[h * Q_BLOCK:(h + 1) * Q_BLOCK], o)
    return o.astype(BF16)


def _quad_kernel(*refs, mode, n_lat, nb, per_step):
    if mode == "na":
        q_ref, k_ref, v_ref = refs[1:4]
        bias_refs, o_ref = refs[4:4 + per_step], refs[-1]
    else:
        q_ref, k_ref, v_ref, sink_ref, o_ref = refs
    i = pl.program_id(1)
    for u in range(per_step):
        rows = slice(u * Q_BLOCK, (u + 1) * Q_BLOCK)
        o_ref[rows, :] = _quad_block(
            q_ref[rows, :], per_step * i + u, k_ref, v_ref,
            bias_refs[u][0, 0] if mode == "na" else None,
            sink_ref[0] if mode == "win" else None,
            mode=mode, n_lat=n_lat, nb=nb)


def _rpb_kernel(r_ref, e_ref, keep_ref, o_ref):
    r = r_ref[...]
    acc = jnp.zeros(o_ref.shape, F32)
    for dc in range(e_ref.shape[0]):
        acc = acc + r[:, dc:dc + 1] * e_ref[dc:dc + 1, :]
    o_ref[...] = jnp.where(keep_ref[...] > 0, acc, MASKED)


def _na_bias_tables(rpb, n_lat):
    heads, n_dr, n_dc = rpb.shape
    rows = n_lat // GRID_W
    kh = min(NA_KH, rows)
    qc = np.arange(GRID_W)[:, None]
    kc = np.arange(GRID_W)[None, :]
    c0 = np.clip(qc - NA_KW // 2, 0, GRID_W - NA_KW)
    keep = ((kc >= c0) & (kc < c0 + NA_KW)).reshape(1, -1).astype(np.float32)
    onehot = (np.arange(n_dc)[:, None, None] == (kc - qc + NA_KW - 1)[None]).reshape(n_dc, -1).astype(np.float32)
    tiles = pl.pallas_call(
        _rpb_kernel,
        out_shape=jax.ShapeDtypeStruct((heads * n_dr, GRID_W * GRID_W), F32),
        name="rpb_tiles",
    )(rpb.reshape(heads * n_dr, n_dc).astype(F32), jnp.asarray(onehot), jnp.asarray(keep))
    tiles = tiles.reshape(heads, n_dr, GRID_W, GRID_W)

    type_tables, index_of, types = [], {}, []
    for b in range(rows // 2):
        base = int(np.clip(2 * b - NA_KH // 2, 0, rows - NA_KEY_ROWS))
        r = 2 * b + np.arange(2)[:, None]
        kr = base + np.arange(NA_KEY_ROWS)[None, :]
        r0 = np.clip(r - kh // 2, 0, rows - kh)
        valid = (kr >= r0) & (kr < r0 + kh)
        assert (valid.sum(1) == kh).all()
        drow = np.where(valid, kr - r + (NA_KH - 1), -1)
        key = drow.tobytes()
        if key not in index_of:
            index_of[key] = len(type_tables)
            type_tables.append(drow)
        types.append(index_of[key])
    type_tables.append(np.full((2, NA_KEY_ROWS), -1))
    types += [len(type_tables) - 1] * (CTX_LEN // Q_BLOCK)

    masked_tile = jnp.full((heads, GRID_W, GRID_W), MASKED, F32)
    per_type = []
    for drow in type_tables:
        q_rows = [jnp.concatenate([tiles[:, dr] if dr >= 0 else masked_tile for dr in drow[qr]], axis=-1)
                  for qr in range(2)]
        per_type.append(jnp.concatenate(q_rows, axis=1))
    bias = jnp.stack(per_type)
    n_types, nb = bias.shape[0], bias.shape[-1]
    bias = bias.reshape(n_types, heads // 4, 4 * Q_BLOCK, nb)
    return bias, np.asarray(types, np.int32)


def _na_attention(qkv, rpb, *, n_lat):
    n_rows = qkv.shape[0]
    width = qkv.shape[1] // 3
    n_quads = width // QUAD
    nb = NA_KEY_ROWS * GRID_W
    per_step = 2
    bias, types = _na_bias_tables(rpb, n_lat)
    kern = functools.partial(_quad_kernel, mode="na", n_lat=n_lat, nb=nb, per_step=per_step)
    kv_spec = lambda off: pl.BlockSpec((n_rows, QUAD), lambda j, i, t: (0, off + j), pipeline_mode=pl.Buffered(1))
    bias_spec = lambda u: pl.BlockSpec((1, 1, 4 * Q_BLOCK, nb), lambda j, i, t: (t[per_step * i + u], j, 0, 0))
    return pl.pallas_call(
        kern,
        grid_spec=pltpu.PrefetchScalarGridSpec(
            num_scalar_prefetch=1,
            grid=(n_quads, n_rows // (per_step * Q_BLOCK)),
            in_specs=[
                pl.BlockSpec((per_step * Q_BLOCK, QUAD), lambda j, i, t: (i, j)),
                kv_spec(n_quads),
                kv_spec(2 * n_quads),
            ] + [bias_spec(u) for u in range(per_step)],
            out_specs=pl.BlockSpec((per_step * Q_BLOCK, QUAD), lambda j, i, t: (i, j)),
        ),
        out_shape=jax.ShapeDtypeStruct((n_rows, width), BF16),
        compiler_params=_params("arbitrary", "arbitrary"),
        name="na_attn",
    )(jnp.asarray(types), qkv, qkv, qkv, *([bias] * per_step))


def _window_attention(qkv, sink, *, n_lat):
    n_rows = qkv.shape[0]
    n_groups = sink.shape[0] // 4
    width = n_groups * QUAD
    nb = 3 * Q_BLOCK
    per_step = 2
    sink_col = jnp.repeat(sink.astype(F32).reshape(4, n_groups).T, Q_BLOCK, axis=1).reshape(n_groups, 4 * Q_BLOCK, 1)
    kern = functools.partial(_quad_kernel, mode="win", n_lat=n_lat, nb=nb, per_step=per_step)
    kv_spec = lambda off: pl.BlockSpec((n_rows, QUAD), lambda g, i: (0, off), pipeline_mode=pl.Buffered(1))
    return pl.pallas_call(
        kern,
        grid=(n_groups, n_rows // (per_step * Q_BLOCK)),
        in_specs=[
            pl.BlockSpec((per_step * Q_BLOCK, QUAD), lambda g, i: (i, g)),
            kv_spec(n_groups),
            kv_spec(n_groups + 1),
            pl.BlockSpec((1, 4 * Q_BLOCK, 1), lambda g, i: (g, 0, 0)),
        ],
        out_specs=pl.BlockSpec((per_step * Q_BLOCK, QUAD), lambda g, i: (i, g)),
        out_shape=jax.ShapeDtypeStruct((n_rows, width), BF16),
        compiler_params=_params("arbitrary", "arbitrary"),
        name="window_attn",
    )(qkv, qkv, qkv, sink_col)


def _rope_tables(n_lat):
    half = HEAD_DIM // 2
    freqs = ROPE_THETA ** (-jnp.arange(0, half, 2, dtype=F32) / half)
    pos = jnp.arange(n_lat)
    ang_r = (pos // GRID_W).astype(F32)[:, None] * freqs
    ang_c = (pos % GRID_W).astype(F32)[:, None] * freqs
    cos = jnp.concatenate([jnp.cos(ang_r)] * 2 + [jnp.cos(ang_c)] * 2, axis=-1)
    sin = jnp.concatenate([-jnp.sin(ang_r), jnp.sin(ang_r), -jnp.sin(ang_c), jnp.sin(ang_c)], axis=-1)
    cos = jnp.concatenate([jnp.tile(cos, (1, 2)), jnp.ones((CTX_LEN, LANES), F32)], axis=0)
    sin = jnp.concatenate([jnp.tile(sin, (1, 2)), jnp.zeros((CTX_LEN, LANES), F32)], axis=0)
    return cos, sin


def _scale_q(w, q_cols, scale):
    s = jnp.concatenate([jnp.full((q_cols,), scale, F32), jnp.ones((w.shape[1] - q_cols,), F32)])
    return w * s


def kernel(x, c, ctx, c_ctx, ada_w, ada_b, norm_g, ffn_w13, ffn_w2, a_wqkv, a_wo, a_lambda, a_subln,
           b_wqkv, b_wo, b_rpb, c_wqkv, c_wo, c_sink):
    depth = ada_w.shape[0]
    n_lat = x.shape[1]
    d = x.shape[2]
    xa = jnp.concatenate([x[0], ctx[0]], axis=0)
    mods = _mods(c, c_ctx, ada_w, ada_b).reshape(depth, 2, 6, d)
    cos, sin = _rope_tables(n_lat)
    q_scale = HEAD_DIM ** -0.5

    n_heads_c = c_sink.shape[1]
    kv_c = (c_wqkv.shape[2] - n_heads_c * HEAD_DIM) // (2 * HEAD_DIM)
    grp_c = n_heads_c // kv_c
    perm = (np.arange(kv_c)[None, :, None] * grp_c + np.arange(grp_c)[:, None, None]) * HEAD_DIM + np.arange(HEAD_DIM)[None, None, :]
    perm = perm.reshape(-1)

    for i in range(depth):
        mixer, j = i % 3, i // 3
        g = norm_g[i]
        if mixer == 0:
            width = a_wo.shape[1]
            n_heads = width // LANES
            w = _scale_q(a_wqkv[j], width, q_scale * LOG2E).astype(BF16)
            out_slabs = tuple(range(2 * n_heads)) + tuple(2 * n_heads + 2 * h for h in range(n_heads))
            ones_slabs = tuple(2 * n_heads + 2 * h + 1 for h in range(n_heads))
            qkv = _qkv(xa, mods[i], g[0], w, cos, sin, n_lat=n_lat, rope_cols=2 * width, out_slabs=out_slabs,
                       ones_slabs=ones_slabs)
            lam_init = 0.8 - 0.6 * math.exp(-0.3 * i)
            o = _diff_attention(qkv, a_lambda[j], a_subln[j], n_lat=n_lat, n_heads=n_heads, lam_init=lam_init)
            wo = a_wo[j]
        elif mixer == 1:
            width = b_wo.shape[1]
            w = _scale_q(b_wqkv[j], width, q_scale).astype(BF16)
            qkv = _qkv(xa, mods[i], g[0], w, cos, sin, n_lat=n_lat, rope_cols=0)
            o = _na_attention(qkv, b_rpb[j], n_lat=n_lat)
            wo = b_wo[j]
        else:
            width = n_heads_c * HEAD_DIM
            wq = c_wqkv[j]
            w = jnp.concatenate([wq[:, :width][:, perm], wq[:, width:]], axis=1)
            w = _scale_q(w, width, q_scale).astype(BF16)
            qkv = _qkv(xa, mods[i], g[0], w, cos, sin, n_lat=n_lat, rope_cols=width + kv_c * HEAD_DIM)
            o = _window_attention(qkv, c_sink[j], n_lat=n_lat)
            wo = c_wo[j][perm, :]
        xa = _attn_out(o, xa, mods[i], g[1], wo.astype(BF16), n_lat=n_lat)
        xa = _ffn(xa, mods[i], g[2], g[3], ffn_w13[i].astype(BF16), ffn_w2[i].astype(BF16), n_lat=n_lat)
    return xa[:n_lat][None]
```

```python
import functools
import math

import numpy as np
import jax
import jax.numpy as jnp
from jax import lax
from jax.experimental import pallas as pl
from jax.experimental.pallas import tpu as pltpu

F32 = jnp.float32
BF16 = jnp.bfloat16

GRID_W = 64
CTX_LEN = 256
NORM_EPS = 1e-6
ROPE_THETA = 10000.0
HEAD_DIM = 64
NA_KH = 8
NA_KW = 16
NA_KEY_ROWS = 10
WINDOW = 128
Q_BLOCK = 128
LANES = 128
QUAD = 4 * HEAD_DIM
MASKED = -1e30
LOG2E = math.log2(math.e)
VMEM_LIMIT_BYTES = 56 * 1024 * 1024
_NT = (((1,), (1,)), ((), ()))


def _params(*sem, flags=None):
    return pltpu.CompilerParams(dimension_semantics=sem, vmem_limit_bytes=VMEM_LIMIT_BYTES, flags=flags)


def _row_tile(n_rows):
    for tm in (640, 512, 256, 128):
        if n_rows % tm == 0:
            return tm
    raise ValueError(f"no row tile divides {n_rows}")


def _rms(x, g):
    return x * lax.rsqrt(jnp.mean(x * x, axis=-1, keepdims=True) + NORM_EPS) * g


def _is_ctx(tm, n_lat):
    row = pl.program_id(0) * tm + lax.broadcasted_iota(jnp.int32, (tm, 1), 0)
    return row >= n_lat


def _mod(mod_ref, idx, is_ctx):
    return jnp.where(is_ctx, mod_ref[1, idx:idx + 1, :], mod_ref[0, idx:idx + 1, :])


def _mods_kernel(c_ref, w_ref, b_ref, o_ref):
    s = c_ref[...]
    s = s * jax.nn.sigmoid(s)
    w = w_ref[0]
    b = b_ref[0]
    lat = jnp.sum(w * s[:, 0:1], axis=0, keepdims=True) + b
    cx = jnp.sum(w * s[:, 1:2], axis=0, keepdims=True) + b
    o_ref[0] = jnp.concatenate([lat, cx], axis=0)


def _mods(c, c_ctx, ada_w, ada_b):
    depth, d, n = ada_w.shape
    tn = 512
    c2 = jnp.stack([c[0], c_ctx], axis=1)
    return pl.pallas_call(
        _mods_kernel,
        grid=(depth, n // tn),
        in_specs=[
            pl.BlockSpec((d, 2), lambda l, j: (0, 0)),
            pl.BlockSpec((1, d, tn), lambda l, j: (l, 0, j)),
            pl.BlockSpec((1, 1, tn), lambda l, j: (l, 0, j)),
        ],
        out_specs=pl.BlockSpec((1, 2, tn), lambda l, j: (l, 0, j)),
        out_shape=jax.ShapeDtypeStruct((depth, 2, n), F32),
        compiler_params=_params("arbitrary", "arbitrary"),
        name="ada_mods",
    )(c2, ada_w, ada_b.reshape(depth, 1, n))


def _qkv_kernel(x_ref, mod_ref, g_ref, w_ref, cos_ref, sin_ref, o_ref, *, tm, n_lat, rope_cols, col_chunk, out_slabs,
                ones_slabs):
    is_ctx = _is_ctx(tm, n_lat)
    h = _rms(x_ref[...], g_ref[...]) * (1.0 + _mod(mod_ref, 1, is_ctx)) + _mod(mod_ref, 0, is_ctx)
    h = h.astype(BF16)
    lane = lax.broadcasted_iota(jnp.int32, (1, LANES), 1)
    first_half = (lane % 32) < 16
    cos = cos_ref[...]
    sin = sin_ref[...]
    for c0 in range(0, w_ref.shape[1], col_chunk):
        acc = jnp.dot(h, w_ref[:, c0:c0 + col_chunk], preferred_element_type=F32)
        for s0 in range(0, col_chunk, LANES):
            a = acc[:, s0:s0 + LANES]
            if c0 + s0 < rope_cols:
                partner = jnp.where(first_half, pltpu.roll(a, LANES - 16, 1), pltpu.roll(a, 16, 1))
                a = a * cos + partner * sin
            dst = out_slabs[(c0 + s0) // LANES] * LANES
            o_ref[:, dst:dst + LANES] = a.astype(BF16)
    for slab in ones_slabs:
        o_ref[:, slab * LANES:(slab + 1) * LANES] = jnp.ones((tm, LANES), BF16)


def _qkv(xa, mod, g, w, cos, sin, *, n_lat, rope_cols, out_slabs=None, ones_slabs=()):
    n_rows, d = xa.shape
    n_in = w.shape[1] // LANES
    out_slabs = tuple(range(n_in)) if out_slabs is None else out_slabs
    ncols = (n_in + len(ones_slabs)) * LANES
    tm = _row_tile(n_rows)
    kern = functools.partial(_qkv_kernel, tm=tm, n_lat=n_lat, rope_cols=rope_cols, col_chunk=512,
                             out_slabs=out_slabs, ones_slabs=ones_slabs)
    return pl.pallas_call(
        kern,
        grid=(n_rows // tm,),
        in_specs=[
            pl.BlockSpec((tm, d), lambda i: (i, 0)),
            pl.BlockSpec((2, 6, d), lambda i: (0, 0, 0)),
            pl.BlockSpec((1, d), lambda i: (0, 0)),
            pl.BlockSpec(w.shape, lambda i: (0, 0)),
            pl.BlockSpec((tm, LANES), lambda i: (i, 0)),
            pl.BlockSpec((tm, LANES), lambda i: (i, 0)),
        ],
        out_specs=pl.BlockSpec((tm, ncols), lambda i: (i, 0)),
        out_shape=jax.ShapeDtypeStruct((n_rows, ncols), BF16),
        compiler_params=_params("arbitrary"),
        name="norm_qkv",
    )(xa, mod, g.reshape(1, d), w, cos, sin)


def _attn_out_kernel(o_ref, x_ref, mod_ref, g_ref, w_ref, out_ref, *, tm, n_lat):
    is_ctx = _is_ctx(tm, n_lat)
    y = jnp.dot(o_ref[...], w_ref[...], preferred_element_type=F32)
    out_ref[...] = x_ref[...] + _mod(mod_ref, 2, is_ctx) * _rms(y, g_ref[...])


def _attn_out(o, xa, mod, g, w, *, n_lat):
    n_rows, d = xa.shape
    tm = _row_tile(n_rows)
    kern = functools.partial(_attn_out_kernel, tm=tm, n_lat=n_lat)
    return pl.pallas_call(
        kern,
        grid=(n_rows // tm,),
        in_specs=[
            pl.BlockSpec((tm, o.shape[1]), lambda i: (i, 0)),
            pl.BlockSpec((tm, d), lambda i: (i, 0)),
            pl.BlockSpec((2, 6, d), lambda i: (0, 0, 0)),
            pl.BlockSpec((1, d), lambda i: (0, 0)),
            pl.BlockSpec(w.shape, lambda i: (0, 0)),
        ],
        out_specs=pl.BlockSpec((tm, d), lambda i: (i, 0)),
        out_shape=jax.ShapeDtypeStruct((n_rows, d), F32),
        compiler_params=_params("arbitrary"),
        name="attn_out",
    )(o, xa, mod, g.reshape(1, d), w)


def _ffn_kernel(x_ref, mod_ref, g2_ref, g3_ref, w13_ref, w2_ref, out_ref, *, tm, n_lat, hidden, n_chunks):
    is_ctx = _is_ctx(tm, n_lat)
    x = x_ref[...]
    h = _rms(x, g2_ref[...]) * (1.0 + _mod(mod_ref, 4, is_ctx)) + _mod(mod_ref, 3, is_ctx)
    h = h.astype(BF16)
    hc = hidden // n_chunks
    f = None
    for c in range(n_chunks):
        gate = jnp.dot(h, w13_ref[:, c * hc:(c + 1) * hc], preferred_element_type=F32)
        up = jnp.dot(h, w13_ref[:, hidden + c * hc:hidden + (c + 1) * hc], preferred_element_type=F32)
        act = (gate * jax.nn.sigmoid(gate) * up).astype(BF16)
        part = jnp.dot(act, w2_ref[c * hc:(c + 1) * hc, :], preferred_element_type=F32)
        f = part if f is None else f + part
    out_ref[...] = x + _mod(mod_ref, 5, is_ctx) * _rms(f, g3_ref[...])


def _ffn(xa, mod, g2, g3, w13, w2, *, n_lat):
    n_rows, d = xa.shape
    hidden = w2.shape[0]
    tm = _row_tile(n_rows)
    n_chunks = 2 if (hidden // 2) % LANES == 0 else 1
    kern = functools.partial(_ffn_kernel, tm=tm, n_lat=n_lat, hidden=hidden, n_chunks=n_chunks)
    return pl.pallas_call(
        kern,
        grid=(n_rows // tm,),
        in_specs=[
            pl.BlockSpec((tm, d), lambda i: (i, 0)),
            pl.BlockSpec((2, 6, d), lambda i: (0, 0, 0)),
            pl.BlockSpec((1, d), lambda i: (0, 0)),
            pl.BlockSpec((1, d), lambda i: (0, 0)),
            pl.BlockSpec(w13.shape, lambda i: (0, 0), pipeline_mode=pl.Buffered(1)),
            pl.BlockSpec(w2.shape, lambda i: (0, 0), pipeline_mode=pl.Buffered(1)),
        ],
        out_specs=pl.BlockSpec((tm, d), lambda i: (i, 0)),
        out_shape=jax.ShapeDtypeStruct((n_rows, d), F32),
        compiler_params=_params("arbitrary"),
        name="ffn",
    )(xa, mod, g2.reshape(1, d), g3.reshape(1, d), w13, w2)


def _diff_kernel(lam_ref, subln_ref, q_ref, k_ref, v_ref, o_ref, qq_sc, s0_sc, s1_sc, m_sc, l_sc, acc_sc, *, tq, tk,
                 n_lat, ctx_row0, lam_init):
    s_sc = (s0_sc, s1_sc)
    lane = lax.broadcasted_iota(jnp.int32, (1, LANES), 1)
    q = q_ref[...]
    zero = jnp.zeros_like(q)
    qq_sc[...] = jnp.concatenate([jnp.where(lane < HEAD_DIM, q, zero), jnp.where(lane >= HEAD_DIM, q, zero)], axis=0)
    m_sc[...] = jnp.full(m_sc.shape, -jnp.inf, F32)
    l_sc[...] = jnp.zeros(l_sc.shape, F32)
    acc_sc[...] = jnp.zeros(acc_sc.shape, F32)

    def scores(row0, size):
        return lax.dot_general(qq_sc[...], k_ref[pl.ds(row0, size), :], _NT, preferred_element_type=F32)

    def softmax_values(s, row0, size):
        m_prev = m_sc[...]
        m_next = jnp.maximum(m_prev, jnp.max(s, axis=1)[:, None])
        m_sc[...] = m_next
        p = jnp.exp2(s - jnp.tile(m_next, (1, size // LANES))).astype(BF16)
        alpha = jnp.exp2(m_prev - m_next)
        pv = jnp.dot(p, v_ref[pl.ds(row0, size), :], preferred_element_type=F32)
        acc_sc[...] = alpha * acc_sc[...] + pv[:, :LANES]
        l_sc[...] = alpha * l_sc[...] + pv[:, LANES:]

    if n_lat == 0:
        softmax_values(scores(ctx_row0, CTX_LEN), ctx_row0, CTX_LEN)
    else:
        n_chunks = n_lat // tk

        def qk(t, slot):
            s_sc[slot][...] = scores(pl.multiple_of(t * tk, tk), tk)

        def smpv(t, slot):
            softmax_values(s_sc[slot][...], pl.multiple_of(t * tk, tk), tk)

        s1_sc[:, :CTX_LEN] = scores(ctx_row0, CTX_LEN)
        softmax_values(s1_sc[:, :CTX_LEN], ctx_row0, CTX_LEN); qk(0, 0)

        def body(u, carry):
            t = 2 * u
            smpv(t, 0); qk(t + 1, 1)
            smpv(t + 1, 1); qk(t + 2, 0)
            return carry

        lax.fori_loop(0, (n_chunks - 2) // 2, body, 0)
        smpv(n_chunks - 2, 0); qk(n_chunks - 1, 1)
        smpv(n_chunks - 1, 1)

    lam = lam_ref[...]
    lam_full = (jnp.exp(jnp.sum(lam[0:1] * lam[1:2], axis=1, keepdims=True))
                - jnp.exp(jnp.sum(lam[2:3] * lam[3:4], axis=1, keepdims=True)) + lam_init)
    o2 = acc_sc[...] / l_sc[...]
    o = o2[:tq] - lam_full * o2[tq:]
    o = _rms(o, subln_ref[...]) * (1.0 - lam_init)
    o_ref[...] = o.astype(BF16)


def _diff_call(qkv, lam, subln, *, n_heads, lam_init, tq, tk, n_lat, q_block0, n_q_blocks, kv_rows, kv_block0):
    kern = functools.partial(_diff_kernel, tq=tq, tk=tk, n_lat=n_lat, ctx_row0=n_lat, lam_init=lam_init)
    return pl.pallas_call(
        kern,
        grid=(n_heads, n_q_blocks),
        in_specs=[
            pl.BlockSpec(lam.shape, lambda h, i: (0, 0)),
            pl.BlockSpec((1, LANES), lambda h, i: (0, 0)),
            pl.BlockSpec((tq, LANES), lambda h, i: (q_block0 + i, h)),
            pl.BlockSpec((kv_rows, LANES), lambda h, i: (kv_block0, n_heads + h), pipeline_mode=pl.Buffered(1)),
            pl.BlockSpec((kv_rows, 2 * LANES), lambda h, i: (kv_block0, n_heads + h), pipeline_mode=pl.Buffered(1)),
        ],
        out_specs=pl.BlockSpec((tq, LANES), lambda h, i: (i, h)),
        out_shape=jax.ShapeDtypeStruct((n_q_blocks * tq, n_heads * LANES), BF16),
        scratch_shapes=[
            pltpu.VMEM((2 * tq, LANES), BF16),
            pltpu.VMEM((2 * tq, tk), F32), pltpu.VMEM((2 * tq, tk), F32),
            pltpu.VMEM((2 * tq, LANES), F32),
            pltpu.VMEM((2 * tq, LANES), F32),
            pltpu.VMEM((2 * tq, LANES), F32),
        ],
        compiler_params=_params("arbitrary", "arbitrary"),
        name="diff_attn",
    )(lam, subln.reshape(1, -1), qkv, qkv, qkv)


def _diff_attention(qkv, lam, subln, *, n_lat, n_heads, lam_init):
    n_rows = qkv.shape[0]
    tq = 512 if n_lat % 512 == 0 else 256
    tk = 1024 if n_lat % 2048 == 0 else 256
    assert (n_lat // tk) % 2 == 0 and n_lat % CTX_LEN == 0
    common = dict(n_heads=n_heads, lam_init=lam_init)
    o_lat = _diff_call(qkv, lam, subln, tq=tq, tk=tk, n_lat=n_lat, q_block0=0, n_q_blocks=n_lat // tq,
                       kv_rows=n_rows, kv_block0=0, **common)
    o_ctx = _diff_call(qkv, lam, subln, tq=CTX_LEN, tk=CTX_LEN, n_lat=0, q_block0=n_lat // CTX_LEN, n_q_blocks=1,
                       kv_rows=CTX_LEN, kv_block0=n_lat // CTX_LEN, **common)
    return jnp.concatenate([o_lat, o_ctx], axis=0)


def _quad_kernel(types_ref, q_ref, k_ref, v_ref, bias_ref, *rest, n_lat, nb, mode, blocks):
    if mode == "win":
        sink_ref, o_ref, s0_sc, s1_sc = rest
    else:
        o_ref, s0_sc, s1_sc = rest
    s_sc = (s0_sc, s1_sc)
    i = pl.program_id(1)
    lane = lax.broadcasted_iota(jnp.int32, (1, QUAD), 1)

    def rows_of(u):
        return pl.ds(pl.multiple_of(u * Q_BLOCK, Q_BLOCK), Q_BLOCK)

    def key_start(blk):
        if mode == "na":
            base = jnp.clip(2 * blk - NA_KH // 2, 0, n_lat // GRID_W - NA_KEY_ROWS)
            return pl.multiple_of(base * GRID_W, GRID_W)
        return pl.multiple_of(jnp.clip((blk - 1) * Q_BLOCK, 0, n_lat - nb), Q_BLOCK)

    def qk(u, slot):
        blk = i * blocks + u
        q = q_ref[rows_of(u), :]
        zero = jnp.zeros_like(q)
        qq = jnp.concatenate([jnp.where((lane // HEAD_DIM) == h, q, zero) for h in range(4)], axis=0)
        sb = lax.dot_general(qq, k_ref[pl.ds(key_start(blk), nb), :], _NT, preferred_element_type=F32)
        s_sc[slot][:, :nb] = sb + bias_ref[types_ref[blk], 0]
        s_sc[slot][:, nb:] = lax.dot_general(qq, k_ref[n_lat:n_lat + CTX_LEN, :], _NT, preferred_element_type=F32)

    def smpv(u, slot):
        start = key_start(i * blocks + u)
        s = s_sc[slot][...]
        m = jnp.max(s, axis=1, keepdims=True)
        if mode == "win":
            sink = sink_ref[0] * LOG2E
            m = jnp.maximum(m, sink)
        p = jnp.exp2(s - m)
        l = jnp.sum(p, axis=1, keepdims=True)
        if mode == "win":
            l = l + jnp.exp2(sink - m)
        p = p.astype(BF16)
        pv = (jnp.dot(p[:, :nb], v_ref[pl.ds(start, nb), :], preferred_element_type=F32)
              + jnp.dot(p[:, nb:], v_ref[n_lat:n_lat + CTX_LEN, :], preferred_element_type=F32)) / l
        o = jnp.zeros((Q_BLOCK, QUAD), F32)
        for h in range(4):
            o = jnp.where((lane // HEAD_DIM) == h, pv[h * Q_BLOCK:(h + 1) * Q_BLOCK], o)
        o_ref[rows_of(u), :] = o.astype(BF16)

    qk(0, 0)

    def body(w, carry):
        u = 2 * w
        smpv(u, 0); qk(u + 1, 1)
        smpv(u + 1, 1); qk(u + 2, 0)
        return carry

    lax.fori_loop(0, (blocks - 2) // 2, body, 0)
    smpv(blocks - 2, 0); qk(blocks - 1, 1)
    smpv(blocks - 1, 1)


def _quad_attention(qkv, bias, types, sink_col, *, n_lat, nb, mode, q_slab, k_slab, v_slab, n_slabs):
    n_rows = qkv.shape[0]
    n_blocks = n_rows // Q_BLOCK
    blocks = next(b for b in (26, 18, 10, 6, 2) if n_blocks % b == 0)
    n_types = bias.shape[0]
    kern = functools.partial(_quad_kernel, n_lat=n_lat, nb=nb, mode=mode, blocks=blocks)
    in_specs = [
        pl.BlockSpec((blocks * Q_BLOCK, QUAD), lambda j, i, t: (i, q_slab(j))),
        pl.BlockSpec((n_rows, QUAD), lambda j, i, t: (0, k_slab(j)), pipeline_mode=pl.Buffered(1)),
        pl.BlockSpec((n_rows, QUAD), lambda j, i, t: (0, v_slab(j)), pipeline_mode=pl.Buffered(1)),
        pl.BlockSpec((n_types, 1, 4 * Q_BLOCK, nb), lambda j, i, t: (0, j if bias.shape[1] > 1 else 0, 0, 0),
                     pipeline_mode=pl.Buffered(1)),
    ]
    args = [jnp.asarray(types), qkv, qkv, qkv, bias]
    if mode == "win":
        in_specs.append(pl.BlockSpec((1, 4 * Q_BLOCK, 1), lambda j, i, t: (j, 0, 0)))
        args.append(sink_col)
    return pl.pallas_call(
        kern,
        grid_spec=pltpu.PrefetchScalarGridSpec(
            num_scalar_prefetch=1,
            grid=(n_slabs, n_blocks // blocks),
            in_specs=in_specs,
            out_specs=pl.BlockSpec((blocks * Q_BLOCK, QUAD), lambda j, i, t: (i, j)),
            scratch_shapes=[pltpu.VMEM((4 * Q_BLOCK, nb + CTX_LEN), F32)] * 2,
        ),
        out_shape=jax.ShapeDtypeStruct((n_rows, n_slabs * QUAD), BF16),
        compiler_params=_params("arbitrary", "arbitrary"),
        name=mode + "_attn",
    )(*args)


def _rpb_kernel(r_ref, e_ref, keep_ref, o_ref):
    r = r_ref[...] * LOG2E
    acc = jnp.zeros(o_ref.shape, F32)
    for dc in range(e_ref.shape[0]):
        acc = acc + r[:, dc:dc + 1] * e_ref[dc:dc + 1, :]
    o_ref[...] = jnp.where(keep_ref[...] > 0, acc, MASKED)


def _na_bias_tables(rpb, n_lat):
    heads, n_dr, n_dc = rpb.shape
    rows = n_lat // GRID_W
    kh = min(NA_KH, rows)
    qc = np.arange(GRID_W)[:, None]
    kc = np.arange(GRID_W)[None, :]
    c0 = np.clip(qc - NA_KW // 2, 0, GRID_W - NA_KW)
    keep = ((kc >= c0) & (kc < c0 + NA_KW)).reshape(1, -1).astype(np.float32)
    onehot = (np.arange(n_dc)[:, None, None] == (kc - qc + NA_KW - 1)[None]).reshape(n_dc, -1).astype(np.float32)
    tiles = pl.pallas_call(
        _rpb_kernel,
        out_shape=jax.ShapeDtypeStruct((heads * n_dr, GRID_W * GRID_W), F32),
        name="rpb_tiles",
    )(rpb.reshape(heads * n_dr, n_dc).astype(F32), jnp.asarray(onehot), jnp.asarray(keep))
    tiles = tiles.reshape(heads, n_dr, GRID_W, GRID_W)

    type_tables, index_of, types = [], {}, []
    for b in range(rows // 2):
        base = int(np.clip(2 * b - NA_KH // 2, 0, rows - NA_KEY_ROWS))
        r = 2 * b + np.arange(2)[:, None]
        kr = base + np.arange(NA_KEY_ROWS)[None, :]
        r0 = np.clip(r - kh // 2, 0, rows - kh)
        valid = (kr >= r0) & (kr < r0 + kh)
        assert (valid.sum(1) == kh).all()
        drow = np.where(valid, kr - r + (NA_KH - 1), -1)
        key = drow.tobytes()
        if key not in index_of:
            index_of[key] = len(type_tables)
            type_tables.append(drow)
        types.append(index_of[key])
    type_tables.append(np.full((2, NA_KEY_ROWS), -1))
    types += [len(type_tables) - 1] * (CTX_LEN // Q_BLOCK)

    masked_tile = jnp.full((heads, GRID_W, GRID_W), MASKED, F32)
    per_type = []
    for drow in type_tables:
        q_rows = [jnp.concatenate([tiles[:, dr] if dr >= 0 else masked_tile for dr in drow[qr]], axis=-1)
                  for qr in range(2)]
        per_type.append(jnp.concatenate(q_rows, axis=1))
    bias = jnp.stack(per_type)
    n_types, nb = bias.shape[0], bias.shape[-1]
    bias = bias.reshape(n_types, heads // 4, 4 * Q_BLOCK, nb)
    return bias, np.asarray(types, np.int32)


def _na_attention(qkv, rpb, *, n_lat):
    n_quads = qkv.shape[1] // (3 * QUAD)
    bias, types = _na_bias_tables(rpb, n_lat)
    return _quad_attention(qkv, bias, types, None, n_lat=n_lat, nb=NA_KEY_ROWS * GRID_W, mode="na", n_slabs=n_quads,
                           q_slab=lambda j: j, k_slab=lambda j: n_quads + j, v_slab=lambda j: 2 * n_quads + j)


def _window_tables(n_lat, nb):
    n_blk = n_lat // Q_BLOCK
    r = (np.arange(4 * Q_BLOCK) % Q_BLOCK)[:, None]
    c = np.arange(nb)[None, :]
    tables, index_of, types = [], {}, []
    for b in range(n_blk):
        d0 = b * Q_BLOCK - int(np.clip((b - 1) * Q_BLOCK, 0, n_lat - nb))
        if d0 not in index_of:
            index_of[d0] = len(tables)
            tables.append(np.where(np.abs(d0 + r - c) <= WINDOW, 0.0, MASKED).astype(np.float32))
        types.append(index_of[d0])
    tables.append(np.full((4 * Q_BLOCK, nb), MASKED, np.float32))
    types += [len(tables) - 1] * (CTX_LEN // Q_BLOCK)
    return np.stack(tables)[:, None], np.asarray(types, np.int32)


def _window_attention(qkv, sink, *, n_lat):
    n_groups = sink.shape[0] // 4
    nb = 3 * Q_BLOCK
    masks, types = _window_tables(n_lat, nb)
    sink_col = jnp.repeat(sink.astype(F32).reshape(4, n_groups).T, Q_BLOCK, axis=1).reshape(n_groups, 4 * Q_BLOCK, 1)
    return _quad_attention(qkv, jnp.asarray(masks), types, sink_col, n_lat=n_lat, nb=nb, mode="win", n_slabs=n_groups,
                           q_slab=lambda g: g, k_slab=lambda g: n_groups, v_slab=lambda g: n_groups + 1)


def _rope_tables(n_lat):
    half = HEAD_DIM // 2
    freqs = ROPE_THETA ** (-jnp.arange(0, half, 2, dtype=F32) / half)
    pos = jnp.arange(n_lat)
    ang_r = (pos // GRID_W).astype(F32)[:, None] * freqs
    ang_c = (pos % GRID_W).astype(F32)[:, None] * freqs
    cos = jnp.concatenate([jnp.cos(ang_r)] * 2 + [jnp.cos(ang_c)] * 2, axis=-1)
    sin = jnp.concatenate([-jnp.sin(ang_r), jnp.sin(ang_r), -jnp.sin(ang_c), jnp.sin(ang_c)], axis=-1)
    cos = jnp.concatenate([jnp.tile(cos, (1, 2)), jnp.ones((CTX_LEN, LANES), F32)], axis=0)
    sin = jnp.concatenate([jnp.tile(sin, (1, 2)), jnp.zeros((CTX_LEN, LANES), F32)], axis=0)
    return cos, sin


def _scale_q(w, q_cols, scale):
    s = jnp.concatenate([jnp.full((q_cols,), scale, F32), jnp.ones((w.shape[1] - q_cols,), F32)])
    return w * s


def kernel(x, c, ctx, c_ctx, ada_w, ada_b, norm_g, ffn_w13, ffn_w2, a_wqkv, a_wo, a_lambda, a_subln,
           b_wqkv, b_wo, b_rpb, c_wqkv, c_wo, c_sink):
    depth = ada_w.shape[0]
    n_lat = x.shape[1]
    d = x.shape[2]
    xa = jnp.concatenate([x[0], ctx[0]], axis=0)
    mods = _mods(c, c_ctx, ada_w, ada_b).reshape(depth, 2, 6, d)
    cos, sin = _rope_tables(n_lat)
    q_scale = HEAD_DIM ** -0.5

    n_heads_c = c_sink.shape[1]
    kv_c = (c_wqkv.shape[2] - n_heads_c * HEAD_DIM) // (2 * HEAD_DIM)
    grp_c = n_heads_c // kv_c
    perm = (np.arange(kv_c)[None, :, None] * grp_c + np.arange(grp_c)[:, None, None]) * HEAD_DIM + np.arange(HEAD_DIM)[None, None, :]
    perm = perm.reshape(-1)

    for i in range(depth):
        mixer, j = i % 3, i // 3
        g = norm_g[i]
        if mixer == 0:
            width = a_wo.shape[1]
            n_heads = width // LANES
            w = _scale_q(a_wqkv[j], width, q_scale * LOG2E).astype(BF16)
            out_slabs = tuple(range(2 * n_heads)) + tuple(2 * n_heads + 2 * h for h in range(n_heads))
            ones_slabs = tuple(2 * n_heads + 2 * h + 1 for h in range(n_heads))
            qkv = _qkv(xa, mods[i], g[0], w, cos, sin, n_lat=n_lat, rope_cols=2 * width, out_slabs=out_slabs,
                       ones_slabs=ones_slabs)
            lam_init = 0.8 - 0.6 * math.exp(-0.3 * i)
            o = _diff_attention(qkv, a_lambda[j], a_subln[j], n_lat=n_lat, n_heads=n_heads, lam_init=lam_init)
            wo = a_wo[j]
        elif mixer == 1:
            width = b_wo.shape[1]
            w = _scale_q(b_wqkv[j], width, q_scale * LOG2E).astype(BF16)
            qkv = _qkv(xa, mods[i], g[0], w, cos, sin, n_lat=n_lat, rope_cols=0)
            o = _na_attention(qkv, b_rpb[j], n_lat=n_lat)
            wo = b_wo[j]
        else:
            width = n_heads_c * HEAD_DIM
            wq = c_wqkv[j]
            w = jnp.concatenate([wq[:, :width][:, perm], wq[:, width:]], axis=1)
            w = _scale_q(w, width, q_scale * LOG2E).astype(BF16)
            qkv = _qkv(xa, mods[i], g[0], w, cos, sin, n_lat=n_lat, rope_cols=width + kv_c * HEAD_DIM)
            o = _window_attention(qkv, c_sink[j], n_lat=n_lat)
            wo = c_wo[j][perm, :]
        xa = _attn_out(o, xa, mods[i], g[1], wo.astype(BF16), n_lat=n_lat)
        xa = _ffn(xa, mods[i], g[2], g[3], ffn_w13[i].astype(BF16), ffn_w2[i].astype(BF16), n_lat=n_lat)
    return xa[:n_lat][None]
```

```python
import functools
import math

import numpy as np
import jax
import jax.numpy as jnp
from jax import lax
from jax.experimental import pallas as pl
from jax.experimental.pallas import tpu as pltpu

F32 = jnp.float32
BF16 = jnp.bfloat16

GRID_W = 64
CTX_LEN = 256
NORM_EPS = 1e-6
ROPE_THETA = 10000.0
HEAD_DIM = 64
NA_KH = 8
NA_KW = 16
NA_KEY_ROWS = 10
WINDOW = 128
Q_BLOCK = 128
LANES = 128
QUAD = 4 * HEAD_DIM
MASKED = -1e30
LOG2E = math.log2(math.e)
VMEM_LIMIT_BYTES = 56 * 1024 * 1024
_NT = (((1,), (1,)), ((), ()))


def _params(*sem, flags=None):
    return pltpu.CompilerParams(dimension_semantics=sem, vmem_limit_bytes=VMEM_LIMIT_BYTES, flags=flags)


def _row_tile(n_rows):
    for tm in (640, 512, 256, 128):
        if n_rows % tm == 0:
            return tm
    raise ValueError(f"no row tile divides {n_rows}")


def _rms(x, g):
    return x * lax.rsqrt(jnp.mean(x * x, axis=-1, keepdims=True) + NORM_EPS) * g


def _is_ctx(tm, n_lat):
    row = pl.program_id(0) * tm + lax.broadcasted_iota(jnp.int32, (tm, 1), 0)
    return row >= n_lat


def _mod(mod_ref, idx, is_ctx):
    return jnp.where(is_ctx, mod_ref[1, idx:idx + 1, :], mod_ref[0, idx:idx + 1, :])


def _mods_kernel(c_ref, w_ref, b_ref, o_ref):
    s = c_ref[...]
    s = s * jax.nn.sigmoid(s)
    w = w_ref[0]
    b = b_ref[0]
    lat = jnp.sum(w * s[:, 0:1], axis=0, keepdims=True) + b
    cx = jnp.sum(w * s[:, 1:2], axis=0, keepdims=True) + b
    o_ref[0] = jnp.concatenate([lat, cx], axis=0)


def _mods(c, c_ctx, ada_w, ada_b):
    depth, d, n = ada_w.shape
    tn = 512
    c2 = jnp.stack([c[0], c_ctx], axis=1)
    return pl.pallas_call(
        _mods_kernel,
        grid=(depth, n // tn),
        in_specs=[
            pl.BlockSpec((d, 2), lambda l, j: (0, 0)),
            pl.BlockSpec((1, d, tn), lambda l, j: (l, 0, j)),
            pl.BlockSpec((1, 1, tn), lambda l, j: (l, 0, j)),
        ],
        out_specs=pl.BlockSpec((1, 2, tn), lambda l, j: (l, 0, j)),
        out_shape=jax.ShapeDtypeStruct((depth, 2, n), F32),
        compiler_params=_params("arbitrary", "arbitrary"),
        name="ada_mods",
    )(c2, ada_w, ada_b.reshape(depth, 1, n))


def _qkv_kernel(x_ref, mod_ref, g_ref, w_ref, cos_ref, sin_ref, o_ref, *, tm, n_lat, rope_cols, col_chunk, out_slabs,
                ones_slabs):
    is_ctx = _is_ctx(tm, n_lat)
    h = _rms(x_ref[...], g_ref[...]) * (1.0 + _mod(mod_ref, 1, is_ctx)) + _mod(mod_ref, 0, is_ctx)
    h = h.astype(BF16)
    lane = lax.broadcasted_iota(jnp.int32, (1, LANES), 1)
    first_half = (lane % 32) < 16
    cos = cos_ref[...]
    sin = sin_ref[...]
    for c0 in range(0, w_ref.shape[1], col_chunk):
        acc = jnp.dot(h, w_ref[:, c0:c0 + col_chunk], preferred_element_type=F32)
        for s0 in range(0, col_chunk, LANES):
            a = acc[:, s0:s0 + LANES]
            if c0 + s0 < rope_cols:
                partner = jnp.where(first_half, pltpu.roll(a, LANES - 16, 1), pltpu.roll(a, 16, 1))
                a = a * cos + partner * sin
            dst = out_slabs[(c0 + s0) // LANES] * LANES
            o_ref[:, dst:dst + LANES] = a.astype(BF16)
    for slab in ones_slabs:
        o_ref[:, slab * LANES:(slab + 1) * LANES] = jnp.ones((tm, LANES), BF16)


def _qkv(xa, mod, g, w, cos, sin, *, n_lat, rope_cols, out_slabs=None, ones_slabs=()):
    n_rows, d = xa.shape
    n_in = w.shape[1] // LANES
    out_slabs = tuple(range(n_in)) if out_slabs is None else out_slabs
    ncols = (n_in + len(ones_slabs)) * LANES
    tm = _row_tile(n_rows)
    kern = functools.partial(_qkv_kernel, tm=tm, n_lat=n_lat, rope_cols=rope_cols, col_chunk=512,
                             out_slabs=out_slabs, ones_slabs=ones_slabs)
    return pl.pallas_call(
        kern,
        grid=(n_rows // tm,),
        in_specs=[
            pl.BlockSpec((tm, d), lambda i: (i, 0)),
            pl.BlockSpec((2, 6, d), lambda i: (0, 0, 0)),
            pl.BlockSpec((1, d), lambda i: (0, 0)),
            pl.BlockSpec(w.shape, lambda i: (0, 0)),
            pl.BlockSpec((tm, LANES), lambda i: (i, 0)),
            pl.BlockSpec((tm, LANES), lambda i: (i, 0)),
        ],
        out_specs=pl.BlockSpec((tm, ncols), lambda i: (i, 0)),
        out_shape=jax.ShapeDtypeStruct((n_rows, ncols), BF16),
        compiler_params=_params("arbitrary"),
        name="norm_qkv",
    )(xa, mod, g.reshape(1, d), w, cos, sin)


def _post_attn_kernel(o_ref, x_ref, mod_ref, g_ref, wo_ref, w13_ref, w2_ref, out_ref, *, tm, n_lat, hidden, n_chunks):
    is_ctx = _is_ctx(tm, n_lat)
    y = jnp.dot(o_ref[...], wo_ref[...], preferred_element_type=F32)
    x = x_ref[...] + _mod(mod_ref, 2, is_ctx) * _rms(y, g_ref[1:2, :])
    h = _rms(x, g_ref[2:3, :]) * (1.0 + _mod(mod_ref, 4, is_ctx)) + _mod(mod_ref, 3, is_ctx)
    h = h.astype(BF16)
    hc = hidden // n_chunks
    f = None
    for c in range(n_chunks):
        gate = jnp.dot(h, w13_ref[0, :, c * hc:(c + 1) * hc], preferred_element_type=F32)
        up = jnp.dot(h, w13_ref[0, :, hidden + c * hc:hidden + (c + 1) * hc], preferred_element_type=F32)
        act = (gate * jax.nn.sigmoid(gate) * up).astype(BF16)
        part = jnp.dot(act, w2_ref[0, c * hc:(c + 1) * hc, :], preferred_element_type=F32)
        f = part if f is None else f + part
    out_ref[...] = x + _mod(mod_ref, 5, is_ctx) * _rms(f, g_ref[3:4, :])


def _post_attn(o, xa, mod, g, wo, w13_all, w2_all, layer, *, n_lat):
    n_rows, d = xa.shape
    hidden = w2_all.shape[1]
    tm = _row_tile(n_rows)
    n_chunks = 2 if (hidden // 2) % LANES == 0 else 1
    kern = functools.partial(_post_attn_kernel, tm=tm, n_lat=n_lat, hidden=hidden, n_chunks=n_chunks)
    resident = dict(pipeline_mode=pl.Buffered(1))
    return pl.pallas_call(
        kern,
        grid=(n_rows // tm,),
        in_specs=[
            pl.BlockSpec((tm, o.shape[1]), lambda i: (i, 0)),
            pl.BlockSpec((tm, d), lambda i: (i, 0)),
            pl.BlockSpec((2, 6, d), lambda i: (0, 0, 0)),
            pl.BlockSpec(g.shape, lambda i: (0, 0)),
            pl.BlockSpec(wo.shape, lambda i: (0, 0), **resident),
            pl.BlockSpec((1,) + w13_all.shape[1:], lambda i: (layer, 0, 0), **resident),
            pl.BlockSpec((1,) + w2_all.shape[1:], lambda i: (layer, 0, 0), **resident),
        ],
        out_specs=pl.BlockSpec((tm, d), lambda i: (i, 0)),
        out_shape=jax.ShapeDtypeStruct((n_rows, d), F32),
        compiler_params=_params("arbitrary"),
        name="attn_out_ffn",
    )(o, xa, mod, g, wo, w13_all, w2_all)


def _diff_kernel(lam_ref, subln_ref, q_ref, k_ref, v_ref, o_ref, qq_sc, sctx_sc, s0_sc, s1_sc, m_sc, l_sc, acc_sc, *, tq,
                 tk, tiles, pairs, n_lat, ctx_row0, lam_init):
    s_sc = (s0_sc, s1_sc)
    lane = lax.broadcasted_iota(jnp.int32, (1, LANES), 1)
    lam = lam_ref[...]
    lam_full = (jnp.exp(jnp.sum(lam[0:1] * lam[1:2], axis=1, keepdims=True))
                - jnp.exp(jnp.sum(lam[2:3] * lam[3:4], axis=1, keepdims=True)) + lam_init)

    for g in range(tiles):
        rows = slice(g * tq, (g + 1) * tq)
        q = q_ref[rows, :]
        zero = jnp.zeros_like(q)
        qq_sc[g] = jnp.concatenate([jnp.where(lane < HEAD_DIM, q, zero), jnp.where(lane >= HEAD_DIM, q, zero)], axis=0)
        m_sc[g] = jnp.full(m_sc.shape[1:], -jnp.inf, F32)
        l_sc[g] = jnp.zeros(l_sc.shape[1:], F32)
        acc_sc[g] = jnp.zeros(acc_sc.shape[1:], F32)

        def scores(row0, size, g=g):
            return lax.dot_general(qq_sc[g], k_ref[pl.ds(row0, size), :], _NT, preferred_element_type=F32)

        def softmax_values(s, row0, size, g=g):
            m_prev = m_sc[g]
            m_next = jnp.maximum(m_prev, jnp.max(s, axis=1)[:, None])
            m_sc[g] = m_next
            p = jnp.exp2(s - jnp.tile(m_next, (1, size // LANES))).astype(BF16)
            alpha = jnp.exp2(m_prev - m_next)
            pv = jnp.dot(p, v_ref[pl.ds(row0, size), :], preferred_element_type=F32)
            acc_sc[g] = alpha * acc_sc[g] + pv[:, :LANES]
            l_sc[g] = alpha * l_sc[g] + pv[:, LANES:]

        if n_lat == 0:
            softmax_values(scores(ctx_row0, CTX_LEN), ctx_row0, CTX_LEN)
        else:
            n_chunks = n_lat // tk

            def pair(t, parity, scores=scores, softmax_values=softmax_values):
                softmax_values(s_sc[parity][...], pl.multiple_of(t * tk, tk), tk)
                s_sc[1 - parity][...] = scores(pl.multiple_of((t + 1) * tk, tk), tk)

            sctx_sc[...] = scores(ctx_row0, CTX_LEN)
            softmax_values(sctx_sc[...], ctx_row0, CTX_LEN)
            s_sc[0][...] = scores(0, tk)

            n_trips = (n_chunks - 1) // pairs

            def body(u, carry, pair=pair):
                for j in range(pairs):
                    pair(u * pairs + j, j % 2)
                return carry

            if n_trips > 0:
                lax.fori_loop(0, n_trips, body, 0)
            for t in range(n_trips * pairs, n_chunks - 1):
                pair(t, t % 2)
            softmax_values(s_sc[(n_chunks - 1) % 2][...], (n_chunks - 1) * tk, tk)

        o2 = acc_sc[g] / l_sc[g]
        o = o2[:tq] - lam_full * o2[tq:]
        o = _rms(o, subln_ref[...]) * (1.0 - lam_init)
        o_ref[rows, :] = o.astype(BF16)


def _diff_call(qkv, lam, subln, *, n_heads, lam_init, tq, tk, tiles, n_lat, q_block0, n_q_blocks, kv_rows, kv_block0):
    kern = functools.partial(_diff_kernel, tq=tq, tk=tk, tiles=tiles, pairs=4, n_lat=n_lat, ctx_row0=n_lat,
                             lam_init=lam_init)
    return pl.pallas_call(
        kern,
        grid=(n_heads, n_q_blocks),
        in_specs=[
            pl.BlockSpec(lam.shape, lambda h, i: (0, 0)),
            pl.BlockSpec((1, LANES), lambda h, i: (0, 0)),
            pl.BlockSpec((tiles * tq, LANES), lambda h, i: (q_block0 + i, h)),
            pl.BlockSpec((kv_rows, LANES), lambda h, i: (kv_block0, n_heads + h), pipeline_mode=pl.Buffered(1)),
            pl.BlockSpec((kv_rows, 2 * LANES), lambda h, i: (kv_block0, n_heads + h), pipeline_mode=pl.Buffered(1)),
        ],
        out_specs=pl.BlockSpec((tiles * tq, LANES), lambda h, i: (i, h)),
        out_shape=jax.ShapeDtypeStruct((n_q_blocks * tiles * tq, n_heads * LANES), BF16),
        scratch_shapes=[
            pltpu.VMEM((tiles, 2 * tq, LANES), BF16),
            pltpu.VMEM((2 * tq, CTX_LEN), F32),
            pltpu.VMEM((2 * tq, tk), F32), pltpu.VMEM((2 * tq, tk), F32),
            pltpu.VMEM((tiles, 2 * tq, LANES), F32),
            pltpu.VMEM((tiles, 2 * tq, LANES), F32),
            pltpu.VMEM((tiles, 2 * tq, LANES), F32),
        ],
        compiler_params=_params("arbitrary", "arbitrary"),
        name="diff_attn",
    )(lam, subln.reshape(1, -1), qkv, qkv, qkv)


def _diff_attention(qkv, lam, subln, *, n_lat, n_heads, lam_init):
    n_rows = qkv.shape[0]
    tq = 512 if n_lat % 1024 == 0 else 256
    tk = 1024 if n_lat % 2048 == 0 else 256
    tiles = 2
    assert n_lat % (tiles * tq) == 0 and n_lat % tk == 0 and n_lat % CTX_LEN == 0
    common = dict(n_heads=n_heads, lam_init=lam_init)
    o_lat = _diff_call(qkv, lam, subln, tq=tq, tk=tk, tiles=tiles, n_lat=n_lat, q_block0=0,
                       n_q_blocks=n_lat // (tiles * tq), kv_rows=n_rows, kv_block0=0, **common)
    o_ctx = _diff_call(qkv, lam, subln, tq=CTX_LEN, tk=CTX_LEN, tiles=1, n_lat=0, q_block0=n_lat // CTX_LEN,
                       n_q_blocks=1, kv_rows=CTX_LEN, kv_block0=n_lat // CTX_LEN, **common)
    return jnp.concatenate([o_lat, o_ctx], axis=0)


def _quad_kernel(types_ref, q_ref, k_ref, v_ref, bias_ref, *rest, n_lat, nb, mode, blocks):
    if mode == "win":
        sink_ref, o_ref, s0_sc, s1_sc = rest
    else:
        o_ref, s0_sc, s1_sc = rest
    s_sc = (s0_sc, s1_sc)
    i = pl.program_id(1)
    lane = lax.broadcasted_iota(jnp.int32, (1, QUAD), 1)

    def rows_of(u):
        return pl.ds(pl.multiple_of(u * Q_BLOCK, Q_BLOCK), Q_BLOCK)

    def key_start(blk):
        if mode == "na":
            base = jnp.clip(2 * blk - NA_KH // 2, 0, n_lat // GRID_W - NA_KEY_ROWS)
            return pl.multiple_of(base * GRID_W, GRID_W)
        return pl.multiple_of(jnp.clip((blk - 1) * Q_BLOCK, 0, n_lat - nb), Q_BLOCK)

    def qk(u, slot):
        blk = i * blocks + u
        q = q_ref[rows_of(u), :]
        zero = jnp.zeros_like(q)
        qq = jnp.concatenate([jnp.where((lane // HEAD_DIM) == h, q, zero) for h in range(4)], axis=0)
        sb = lax.dot_general(qq, k_ref[pl.ds(key_start(blk), nb), :], _NT, preferred_element_type=F32)
        s_sc[slot][:, :nb] = sb + bias_ref[types_ref[blk], 0]
        s_sc[slot][:, nb:] = lax.dot_general(qq, k_ref[n_lat:n_lat + CTX_LEN, :], _NT, preferred_element_type=F32)

    def smpv(u, slot):
        start = key_start(i * blocks + u)
        s = s_sc[slot][...]
        m = jnp.max(s, axis=1, keepdims=True)
        if mode == "win":
            sink = sink_ref[0] * LOG2E
            m = jnp.maximum(m, sink)
        p = jnp.exp2(s - m)
        l = jnp.sum(p, axis=1, keepdims=True)
        if mode == "win":
            l = l + jnp.exp2(sink - m)
        p = p.astype(BF16)
        pv = (jnp.dot(p[:, :nb], v_ref[pl.ds(start, nb), :], preferred_element_type=F32)
              + jnp.dot(p[:, nb:], v_ref[n_lat:n_lat + CTX_LEN, :], preferred_element_type=F32)) / l
        o = jnp.zeros((Q_BLOCK, QUAD), F32)
        for h in range(4):
            o = jnp.where((lane // HEAD_DIM) == h, pv[h * Q_BLOCK:(h + 1) * Q_BLOCK], o)
        o_ref[rows_of(u), :] = o.astype(BF16)

    qk(0, 0)

    def body(w, carry):
        u = 2 * w
        smpv(u, 0); qk(u + 1, 1)
        smpv(u + 1, 1); qk(u + 2, 0)
        return carry

    lax.fori_loop(0, (blocks - 2) // 2, body, 0)
    smpv(blocks - 2, 0); qk(blocks - 1, 1)
    smpv(blocks - 1, 1)


def _quad_attention(qkv, bias, types, sink_col, *, n_lat, nb, mode, q_slab, k_slab, v_slab, n_slabs):
    n_rows = qkv.shape[0]
    n_blocks = n_rows // Q_BLOCK
    blocks = next(b for b in (26, 18, 10, 6, 2) if n_blocks % b == 0)
    n_types = bias.shape[0]
    kern = functools.partial(_quad_kernel, n_lat=n_lat, nb=nb, mode=mode, blocks=blocks)
    in_specs = [
        pl.BlockSpec((blocks * Q_BLOCK, QUAD), lambda j, i, t: (i, q_slab(j))),
        pl.BlockSpec((n_rows, QUAD), lambda j, i, t: (0, k_slab(j)), pipeline_mode=pl.Buffered(1)),
        pl.BlockSpec((n_rows, QUAD), lambda j, i, t: (0, v_slab(j)), pipeline_mode=pl.Buffered(1)),
        pl.BlockSpec((n_types, 1, 4 * Q_BLOCK, nb), lambda j, i, t: (0, j if bias.shape[1] > 1 else 0, 0, 0),
                     pipeline_mode=pl.Buffered(1)),
    ]
    args = [jnp.asarray(types), qkv, qkv, qkv, bias]
    if mode == "win":
        in_specs.append(pl.BlockSpec((1, 4 * Q_BLOCK, 1), lambda j, i, t: (j, 0, 0)))
        args.append(sink_col)
    return pl.pallas_call(
        kern,
        grid_spec=pltpu.PrefetchScalarGridSpec(
            num_scalar_prefetch=1,
            grid=(n_slabs, n_blocks // blocks),
            in_specs=in_specs,
            out_specs=pl.BlockSpec((blocks * Q_BLOCK, QUAD), lambda j, i, t: (i, j)),
            scratch_shapes=[pltpu.VMEM((4 * Q_BLOCK, nb + CTX_LEN), F32)] * 2,
        ),
        out_shape=jax.ShapeDtypeStruct((n_rows, n_slabs * QUAD), BF16),
        compiler_params=_params("arbitrary", "arbitrary"),
        name=mode + "_attn",
    )(*args)


def _rpb_kernel(r_ref, e_ref, keep_ref, o_ref):
    r = r_ref[...] * LOG2E
    acc = jnp.zeros(o_ref.shape, F32)
    for dc in range(e_ref.shape[0]):
        acc = acc + r[:, dc:dc + 1] * e_ref[dc:dc + 1, :]
    o_ref[...] = jnp.where(keep_ref[...] > 0, acc, MASKED)


def _na_bias_tables(rpb, n_lat):
    heads, n_dr, n_dc = rpb.shape
    rows = n_lat // GRID_W
    kh = min(NA_KH, rows)
    qc = np.arange(GRID_W)[:, None]
    kc = np.arange(GRID_W)[None, :]
    c0 = np.clip(qc - NA_KW // 2, 0, GRID_W - NA_KW)
    keep = ((kc >= c0) & (kc < c0 + NA_KW)).reshape(1, -1).astype(np.float32)
    onehot = (np.arange(n_dc)[:, None, None] == (kc - qc + NA_KW - 1)[None]).reshape(n_dc, -1).astype(np.float32)
    tiles = pl.pallas_call(
        _rpb_kernel,
        out_shape=jax.ShapeDtypeStruct((heads * n_dr, GRID_W * GRID_W), F32),
        name="rpb_tiles",
    )(rpb.reshape(heads * n_dr, n_dc).astype(F32), jnp.asarray(onehot), jnp.asarray(keep))
    tiles = tiles.reshape(heads, n_dr, GRID_W, GRID_W)

    type_tables, index_of, types = [], {}, []
    for b in range(rows // 2):
        base = int(np.clip(2 * b - NA_KH // 2, 0, rows - NA_KEY_ROWS))
        r = 2 * b + np.arange(2)[:, None]
        kr = base + np.arange(NA_KEY_ROWS)[None, :]
        r0 = np.clip(r - kh // 2, 0, rows - kh)
        valid = (kr >= r0) & (kr < r0 + kh)
        assert (valid.sum(1) == kh).all()
        drow = np.where(valid, kr - r + (NA_KH - 1), -1)
        key = drow.tobytes()
        if key not in index_of:
            index_of[key] = len(type_tables)
            type_tables.append(drow)
        types.append(index_of[key])
    type_tables.append(np.full((2, NA_KEY_ROWS), -1))
    types += [len(type_tables) - 1] * (CTX_LEN // Q_BLOCK)

    masked_tile = jnp.full((heads, GRID_W, GRID_W), MASKED, F32)
    per_type = []
    for drow in type_tables:
        q_rows = [jnp.concatenate([tiles[:, dr] if dr >= 0 else masked_tile for dr in drow[qr]], axis=-1)
                  for qr in range(2)]
        per_type.append(jnp.concatenate(q_rows, axis=1))
    bias = jnp.stack(per_type)
    n_types, nb = bias.shape[0], bias.shape[-1]
    bias = bias.reshape(n_types, heads // 4, 4 * Q_BLOCK, nb)
    return bias, np.asarray(types, np.int32)


def _na_attention(qkv, rpb, *, n_lat):
    n_quads = qkv.shape[1] // (3 * QUAD)
    bias, types = _na_bias_tables(rpb, n_lat)
    return _quad_attention(qkv, bias, types, None, n_lat=n_lat, nb=NA_KEY_ROWS * GRID_W, mode="na", n_slabs=n_quads,
                           q_slab=lambda j: j, k_slab=lambda j: n_quads + j, v_slab=lambda j: 2 * n_quads + j)


def _window_tables(n_lat, nb):
    n_blk = n_lat // Q_BLOCK
    r = (np.arange(4 * Q_BLOCK) % Q_BLOCK)[:, None]
    c = np.arange(nb)[None, :]
    tables, index_of, types = [], {}, []
    for b in range(n_blk):
        d0 = b * Q_BLOCK - int(np.clip((b - 1) * Q_BLOCK, 0, n_lat - nb))
        if d0 not in index_of:
            index_of[d0] = len(tables)
            tables.append(np.where(np.abs(d0 + r - c) <= WINDOW, 0.0, MASKED).astype(np.float32))
        types.append(index_of[d0])
    tables.append(np.full((4 * Q_BLOCK, nb), MASKED, np.float32))
    types += [len(tables) - 1] * (CTX_LEN // Q_BLOCK)
    return np.stack(tables)[:, None], np.asarray(types, np.int32)


def _window_attention(qkv, sink, *, n_lat):
    n_groups = sink.shape[0] // 4
    nb = 3 * Q_BLOCK
    masks, types = _window_tables(n_lat, nb)
    sink_col = jnp.repeat(sink.astype(F32).reshape(4, n_groups).T, Q_BLOCK, axis=1).reshape(n_groups, 4 * Q_BLOCK, 1)
    return _quad_attention(qkv, jnp.asarray(masks), types, sink_col, n_lat=n_lat, nb=nb, mode="win", n_slabs=n_groups,
                           q_slab=lambda g: g, k_slab=lambda g: n_groups, v_slab=lambda g: n_groups + 1)


def _rope_tables(n_lat):
    half = HEAD_DIM // 2
    freqs = ROPE_THETA ** (-jnp.arange(0, half, 2, dtype=F32) / half)
    pos = jnp.arange(n_lat)
    ang_r = (pos // GRID_W).astype(F32)[:, None] * freqs
    ang_c = (pos % GRID_W).astype(F32)[:, None] * freqs
    cos = jnp.concatenate([jnp.cos(ang_r)] * 2 + [jnp.cos(ang_c)] * 2, axis=-1)
    sin = jnp.concatenate([-jnp.sin(ang_r), jnp.sin(ang_r), -jnp.sin(ang_c), jnp.sin(ang_c)], axis=-1)
    cos = jnp.concatenate([jnp.tile(cos, (1, 2)), jnp.ones((CTX_LEN, LANES), F32)], axis=0)
    sin = jnp.concatenate([jnp.tile(sin, (1, 2)), jnp.zeros((CTX_LEN, LANES), F32)], axis=0)
    return cos, sin


def _scale_q(w, q_cols, scale):
    s = jnp.concatenate([jnp.full((q_cols,), scale, F32), jnp.ones((w.shape[1] - q_cols,), F32)])
    return w * s


def kernel(x, c, ctx, c_ctx, ada_w, ada_b, norm_g, ffn_w13, ffn_w2, a_wqkv, a_wo, a_lambda, a_subln,
           b_wqkv, b_wo, b_rpb, c_wqkv, c_wo, c_sink):
    depth = ada_w.shape[0]
    n_lat = x.shape[1]
    d = x.shape[2]
    xa = jnp.concatenate([x[0], ctx[0]], axis=0)
    mods = _mods(c, c_ctx, ada_w, ada_b).reshape(depth, 2, 6, d)
    cos, sin = _rope_tables(n_lat)
    q_scale = HEAD_DIM ** -0.5
    w13_all = ffn_w13.astype(BF16)
    w2_all = ffn_w2.astype(BF16)

    n_heads_c = c_sink.shape[1]
    kv_c = (c_wqkv.shape[2] - n_heads_c * HEAD_DIM) // (2 * HEAD_DIM)
    grp_c = n_heads_c // kv_c
    perm = (np.arange(kv_c)[None, :, None] * grp_c + np.arange(grp_c)[:, None, None]) * HEAD_DIM + np.arange(HEAD_DIM)[None, None, :]
    perm = perm.reshape(-1)

    for i in range(depth):
        mixer, j = i % 3, i // 3
        g = norm_g[i]
        if mixer == 0:
            width = a_wo.shape[1]
            n_heads = width // LANES
            w = _scale_q(a_wqkv[j], width, q_scale * LOG2E).astype(BF16)
            out_slabs = tuple(range(2 * n_heads)) + tuple(2 * n_heads + 2 * h for h in range(n_heads))
            ones_slabs = tuple(2 * n_heads + 2 * h + 1 for h in range(n_heads))
            qkv = _qkv(xa, mods[i], g[0], w, cos, sin, n_lat=n_lat, rope_cols=2 * width, out_slabs=out_slabs,
                       ones_slabs=ones_slabs)
            lam_init = 0.8 - 0.6 * math.exp(-0.3 * i)
            o = _diff_attention(qkv, a_lambda[j], a_subln[j], n_lat=n_lat, n_heads=n_heads, lam_init=lam_init)
            wo = a_wo[j]
        elif mixer == 1:
            width = b_wo.shape[1]
            w = _scale_q(b_wqkv[j], width, q_scale * LOG2E).astype(BF16)
            qkv = _qkv(xa, mods[i], g[0], w, cos, sin, n_lat=n_lat, rope_cols=0)
            o = _na_attention(qkv, b_rpb[j], n_lat=n_lat)
            wo = b_wo[j]
        else:
            width = n_heads_c * HEAD_DIM
            wq = c_wqkv[j]
            w = jnp.concatenate([wq[:, :width][:, perm], wq[:, width:]], axis=1)
            w = _scale_q(w, width, q_scale * LOG2E).astype(BF16)
            qkv = _qkv(xa, mods[i], g[0], w, cos, sin, n_lat=n_lat, rope_cols=width + kv_c * HEAD_DIM)
            o = _window_attention(qkv, c_sink[j], n_lat=n_lat)
            wo = c_wo[j][perm, :]
        xa = _post_attn(o, xa, mods[i], g, wo.astype(BF16), w13_all, w2_all, i, n_lat=n_lat)
    return xa[:n_lat][None]
```

```python
import functools
import math

import numpy as np
import jax
import jax.numpy as jnp
from jax import lax
from jax.experimental import pallas as pl
from jax.experimental.pallas import tpu as pltpu

F32 = jnp.float32
BF16 = jnp.bfloat16

GRID_W = 64
CTX_LEN = 256
NORM_EPS = 1e-6
ROPE_THETA = 10000.0
HEAD_DIM = 64
NA_KH = 8
NA_KW = 16
NA_KEY_ROWS = 10
WINDOW = 128
Q_BLOCK = 128
LANES = 128
MXU_TILE = 256
QUAD = 4 * HEAD_DIM
MASKED = -1e30
LOG2E = math.log2(math.e)
VMEM_LIMIT_BYTES = 56 * 1024 * 1024
_NT = (((1,), (1,)), ((), ()))


def _params(*sem, flags=None):
    return pltpu.CompilerParams(dimension_semantics=sem, vmem_limit_bytes=VMEM_LIMIT_BYTES, flags=flags)


def _row_tile(n_rows):
    for tm in (640, 512, 256, 128):
        if n_rows % tm == 0:
            return tm
    raise ValueError(f"no row tile divides {n_rows}")


def _rms(x, g):
    return x * lax.rsqrt(jnp.mean(x * x, axis=-1, keepdims=True) + NORM_EPS) * g


def _is_ctx(tm, n_lat):
    row = pl.program_id(0) * tm + lax.broadcasted_iota(jnp.int32, (tm, 1), 0)
    return row >= n_lat


def _mod(mod_ref, idx, is_ctx):
    return jnp.where(is_ctx, mod_ref[1, idx:idx + 1, :], mod_ref[0, idx:idx + 1, :])


def _mods_kernel(c_ref, w_ref, b_ref, o_ref):
    s = c_ref[...]
    s = s * jax.nn.sigmoid(s)
    w = w_ref[0]
    b = b_ref[0]
    lat = jnp.sum(w * s[:, 0:1], axis=0, keepdims=True) + b
    cx = jnp.sum(w * s[:, 1:2], axis=0, keepdims=True) + b
    o_ref[0] = jnp.concatenate([lat, cx], axis=0)


def _mods(c, c_ctx, ada_w, ada_b):
    depth, d, n = ada_w.shape
    tn = 512
    c2 = jnp.stack([c[0], c_ctx], axis=1)
    return pl.pallas_call(
        _mods_kernel,
        grid=(depth, n // tn),
        in_specs=[
            pl.BlockSpec((d, 2), lambda l, j: (0, 0)),
            pl.BlockSpec((1, d, tn), lambda l, j: (l, 0, j)),
            pl.BlockSpec((1, 1, tn), lambda l, j: (l, 0, j)),
        ],
        out_specs=pl.BlockSpec((1, 2, tn), lambda l, j: (l, 0, j)),
        out_shape=jax.ShapeDtypeStruct((depth, 2, n), F32),
        compiler_params=_params("arbitrary", "arbitrary"),
        name="ada_mods",
    )(c2, ada_w, ada_b.reshape(depth, 1, n))


def _qkv_kernel(x_ref, mod_ref, g_ref, w_ref, cos_ref, sin_ref, o_ref, *, tm, n_lat, rope_cols, col_chunk, out_slabs,
                ones_slabs):
    is_ctx = _is_ctx(tm, n_lat)
    h = _rms(x_ref[...], g_ref[...]) * (1.0 + _mod(mod_ref, 1, is_ctx)) + _mod(mod_ref, 0, is_ctx)
    h = h.astype(BF16)
    lane = lax.broadcasted_iota(jnp.int32, (1, LANES), 1)
    first_half = (lane % 32) < 16
    cos = cos_ref[...]
    sin = sin_ref[...]
    for c0 in range(0, w_ref.shape[1], col_chunk):
        acc = jnp.dot(h, w_ref[:, c0:c0 + col_chunk], preferred_element_type=F32)
        for s0 in range(0, col_chunk, LANES):
            a = acc[:, s0:s0 + LANES]
            if c0 + s0 < rope_cols:
                partner = jnp.where(first_half, pltpu.roll(a, LANES - 16, 1), pltpu.roll(a, 16, 1))
                a = a * cos + partner * sin
            dst = out_slabs[(c0 + s0) // LANES] * LANES
            o_ref[:, dst:dst + LANES] = a.astype(BF16)
    for slab in ones_slabs:
        o_ref[:, slab * LANES:(slab + 1) * LANES] = jnp.ones((tm, LANES), BF16)


def _qkv(xa, mod, g, w, cos, sin, *, n_lat, rope_cols, out_slabs=None, ones_slabs=()):
    n_rows, d = xa.shape
    n_in = w.shape[1] // LANES
    out_slabs = tuple(range(n_in)) if out_slabs is None else out_slabs
    ncols = (n_in + len(ones_slabs)) * LANES
    tm = _row_tile(n_rows)
    kern = functools.partial(_qkv_kernel, tm=tm, n_lat=n_lat, rope_cols=rope_cols, col_chunk=512,
                             out_slabs=out_slabs, ones_slabs=ones_slabs)
    return pl.pallas_call(
        kern,
        grid=(n_rows // tm,),
        in_specs=[
            pl.BlockSpec((tm, d), lambda i: (i, 0)),
            pl.BlockSpec((2, 6, d), lambda i: (0, 0, 0)),
            pl.BlockSpec((1, d), lambda i: (0, 0)),
            pl.BlockSpec(w.shape, lambda i: (0, 0)),
            pl.BlockSpec((tm, LANES), lambda i: (i, 0)),
            pl.BlockSpec((tm, LANES), lambda i: (i, 0)),
        ],
        out_specs=pl.BlockSpec((tm, ncols), lambda i: (i, 0)),
        out_shape=jax.ShapeDtypeStruct((n_rows, ncols), BF16),
        compiler_params=_params("arbitrary"),
        name="norm_qkv",
    )(xa, mod, g.reshape(1, d), w, cos, sin)


def _post_attn_kernel(o_ref, x_ref, mod_ref, g_ref, wo_ref, w13_ref, w2_ref, out_ref, *, tm, n_lat, hidden, bounds):
    is_ctx = _is_ctx(tm, n_lat)
    y = jnp.dot(o_ref[...], wo_ref[...], preferred_element_type=F32)
    x = x_ref[...] + _mod(mod_ref, 2, is_ctx) * _rms(y, g_ref[1:2, :])
    h = _rms(x, g_ref[2:3, :]) * (1.0 + _mod(mod_ref, 4, is_ctx)) + _mod(mod_ref, 3, is_ctx)
    h = h.astype(BF16)
    f = None
    for lo, hi in zip(bounds[:-1], bounds[1:]):
        gate = jnp.dot(h, w13_ref[0, :, lo:hi], preferred_element_type=F32)
        up = jnp.dot(h, w13_ref[0, :, hidden + lo:hidden + hi], preferred_element_type=F32)
        act = (gate * jax.nn.sigmoid(gate) * up).astype(BF16)
        part = jnp.dot(act, w2_ref[0, lo:hi, :], preferred_element_type=F32)
        f = part if f is None else f + part
    out_ref[...] = x + _mod(mod_ref, 5, is_ctx) * _rms(f, g_ref[3:4, :])


def _post_attn(o, xa, mod, g, wo, w13_all, w2_all, layer, *, n_lat, out_rows):
    d = xa.shape[1]
    hidden = w2_all.shape[1]
    tm = _row_tile(out_rows)
    mid = -(-hidden // (2 * MXU_TILE)) * MXU_TILE
    bounds = (0, mid, hidden) if 0 < mid < hidden else (0, hidden)
    kern = functools.partial(_post_attn_kernel, tm=tm, n_lat=n_lat, hidden=hidden, bounds=bounds)
    resident = dict(pipeline_mode=pl.Buffered(1))
    return pl.pallas_call(
        kern,
        grid=(out_rows // tm,),
        in_specs=[
            pl.BlockSpec((tm, o.shape[1]), lambda i: (i, 0)),
            pl.BlockSpec((tm, d), lambda i: (i, 0)),
            pl.BlockSpec((2, 6, d), lambda i: (0, 0, 0)),
            pl.BlockSpec(g.shape, lambda i: (0, 0)),
            pl.BlockSpec(wo.shape, lambda i: (0, 0), **resident),
            pl.BlockSpec((1,) + w13_all.shape[1:], lambda i: (layer, 0, 0), **resident),
            pl.BlockSpec((1,) + w2_all.shape[1:], lambda i: (layer, 0, 0), **resident),
        ],
        out_specs=pl.BlockSpec((tm, d), lambda i: (i, 0)),
        out_shape=jax.ShapeDtypeStruct((out_rows, d), F32),
        compiler_params=_params("arbitrary"),
        name="attn_out_ffn",
    )(o, xa, mod, g, wo, w13_all, w2_all)


def _diff_kernel(lam_ref, subln_ref, q_ref, k_ref, v_ref, *rest, tq, tk, tiles, pairs, n_lat, ctx_row0, lam_init):
    o_ref, qq_sc, sctx_sc, s0_sc, s1_sc, m_sc, l_sc, acc_sc = rest[-8:]
    s_sc = (s0_sc, s1_sc)
    lane = lax.broadcasted_iota(jnp.int32, (1, LANES), 1)
    lam = lam_ref[...]
    lam_full = (jnp.exp(jnp.sum(lam[0:1] * lam[1:2], axis=1, keepdims=True))
                - jnp.exp(jnp.sum(lam[2:3] * lam[3:4], axis=1, keepdims=True)) + lam_init)

    for g in range(tiles):
        rows = slice(g * tq, (g + 1) * tq)
        q = q_ref[rows, :]
        zero = jnp.zeros_like(q)
        qq_sc[g] = jnp.concatenate([jnp.where(lane < HEAD_DIM, q, zero), jnp.where(lane >= HEAD_DIM, q, zero)], axis=0)
        m_sc[g] = jnp.full(m_sc.shape[1:], -jnp.inf, F32)
        l_sc[g] = jnp.zeros(l_sc.shape[1:], F32)
        acc_sc[g] = jnp.zeros(acc_sc.shape[1:], F32)

        def scores(row0, size, g=g):
            return lax.dot_general(qq_sc[g], k_ref[pl.ds(row0, size), :], _NT, preferred_element_type=F32)

        def softmax_values(s, row0, size, g=g):
            m_prev = m_sc[g]
            m_next = jnp.maximum(m_prev, jnp.max(s, axis=1)[:, None])
            m_sc[g] = m_next
            p = jnp.exp2(s - jnp.tile(m_next, (1, size // LANES))).astype(BF16)
            alpha = jnp.exp2(m_prev - m_next)
            pv = jnp.dot(p, v_ref[pl.ds(row0, size), :], preferred_element_type=F32)
            acc_sc[g] = alpha * acc_sc[g] + pv[:, :LANES]
            l_sc[g] = alpha * l_sc[g] + pv[:, LANES:]

        if n_lat == 0:
            softmax_values(scores(ctx_row0, CTX_LEN), ctx_row0, CTX_LEN)
        else:
            n_chunks = n_lat // tk

            def pair(t, parity, scores=scores, softmax_values=softmax_values):
                softmax_values(s_sc[parity][...], pl.multiple_of(t * tk, tk), tk)
                s_sc[1 - parity][...] = scores(pl.multiple_of((t + 1) * tk, tk), tk)

            sctx_sc[...] = scores(ctx_row0, CTX_LEN)
            softmax_values(sctx_sc[...], ctx_row0, CTX_LEN)
            s_sc[0][...] = scores(0, tk)

            n_trips = (n_chunks - 1) // pairs

            def body(u, carry, pair=pair):
                for j in range(pairs):
                    pair(u * pairs + j, j % 2)
                return carry

            if n_trips > 0:
                lax.fori_loop(0, n_trips, body, 0)
            for t in range(n_trips * pairs, n_chunks - 1):
                pair(t, t % 2)
            softmax_values(s_sc[(n_chunks - 1) % 2][...], (n_chunks - 1) * tk, tk)

        o2 = acc_sc[g] / l_sc[g]
        o = o2[:tq] - lam_full * o2[tq:]
        o = _rms(o, subln_ref[...]) * (1.0 - lam_init)
        o_ref[rows, :] = o.astype(BF16)


def _diff_call(qkv, lam, subln, *, n_heads, lam_init, tq, tk, tiles, n_lat, q_block0, n_q_blocks, kv_rows, kv_block0,
               out_into=None):
    kern = functools.partial(_diff_kernel, tq=tq, tk=tk, tiles=tiles, pairs=4, n_lat=n_lat, ctx_row0=n_lat,
                             lam_init=lam_init)
    extra_specs, extra_args, aliases = [], [], {}
    if out_into is not None:
        extra_specs, extra_args, aliases = [pl.BlockSpec(memory_space=pl.ANY)], [out_into], {5: 0}
    return pl.pallas_call(
        kern,
        grid=(n_heads, n_q_blocks),
        in_specs=[
            pl.BlockSpec(lam.shape, lambda h, i: (0, 0)),
            pl.BlockSpec((1, LANES), lambda h, i: (0, 0)),
            pl.BlockSpec((tiles * tq, LANES), lambda h, i: (q_block0 + i, h)),
            pl.BlockSpec((kv_rows, LANES), lambda h, i: (kv_block0, n_heads + h), pipeline_mode=pl.Buffered(1)),
            pl.BlockSpec((kv_rows, 2 * LANES), lambda h, i: (kv_block0, n_heads + h), pipeline_mode=pl.Buffered(1)),
        ] + extra_specs,
        out_specs=pl.BlockSpec((tiles * tq, LANES), lambda h, i: (q_block0 + i, h)),
        out_shape=jax.ShapeDtypeStruct((qkv.shape[0], n_heads * LANES), BF16),
        input_output_aliases=aliases,
        scratch_shapes=[
            pltpu.VMEM((tiles, 2 * tq, LANES), BF16),
            pltpu.VMEM((2 * tq, CTX_LEN), F32),
            pltpu.VMEM((2 * tq, tk), F32), pltpu.VMEM((2 * tq, tk), F32),
            pltpu.VMEM((tiles, 2 * tq, LANES), F32),
            pltpu.VMEM((tiles, 2 * tq, LANES), F32),
            pltpu.VMEM((tiles, 2 * tq, LANES), F32),
        ],
        compiler_params=_params("arbitrary", "arbitrary"),
        name="diff_attn",
    )(lam, subln.reshape(1, -1), qkv, qkv, qkv, *extra_args)


def _diff_attention(qkv, lam, subln, *, n_lat, n_heads, lam_init, with_ctx):
    n_rows = qkv.shape[0]
    tq = 512 if n_lat % 1024 == 0 else 256
    tk = 1024 if n_lat % 2048 == 0 else 256
    tiles = 2
    assert n_lat % (tiles * tq) == 0 and n_lat % tk == 0 and n_lat % CTX_LEN == 0
    common = dict(n_heads=n_heads, lam_init=lam_init)
    o_lat = _diff_call(qkv, lam, subln, tq=tq, tk=tk, tiles=tiles, n_lat=n_lat, q_block0=0,
                       n_q_blocks=n_lat // (tiles * tq), kv_rows=n_rows, kv_block0=0, **common)
    if not with_ctx:
        return o_lat
    return _diff_call(qkv, lam, subln, tq=CTX_LEN, tk=CTX_LEN, tiles=1, n_lat=0, q_block0=n_lat // CTX_LEN,
                      n_q_blocks=1, kv_rows=CTX_LEN, kv_block0=n_lat // CTX_LEN, out_into=o_lat, **common)


def _quad_kernel(types_ref, q_ref, k_ref, v_ref, bias_ref, *rest, n_lat, nb, mode, blocks):
    if mode == "win":
        sink_ref, o_ref, s0_sc, s1_sc = rest
    else:
        o_ref, s0_sc, s1_sc = rest
    s_sc = (s0_sc, s1_sc)
    i = pl.program_id(1)
    lane = lax.broadcasted_iota(jnp.int32, (1, QUAD), 1)

    def rows_of(u):
        return pl.ds(pl.multiple_of(u * Q_BLOCK, Q_BLOCK), Q_BLOCK)

    def key_start(blk):
        if mode == "na":
            base = jnp.clip(2 * blk - NA_KH // 2, 0, n_lat // GRID_W - NA_KEY_ROWS)
            return pl.multiple_of(base * GRID_W, GRID_W)
        return pl.multiple_of(jnp.clip((blk - 1) * Q_BLOCK, 0, n_lat - nb), Q_BLOCK)

    def qk(u, slot):
        blk = i * blocks + u
        q = q_ref[rows_of(u), :]
        zero = jnp.zeros_like(q)
        qq = jnp.concatenate([jnp.where((lane // HEAD_DIM) == h, q, zero) for h in range(4)], axis=0)
        sb = lax.dot_general(qq, k_ref[pl.ds(key_start(blk), nb), :], _NT, preferred_element_type=F32)
        s_sc[slot][:, :nb] = sb + bias_ref[types_ref[blk], 0]
        s_sc[slot][:, nb:] = lax.dot_general(qq, k_ref[n_lat:n_lat + CTX_LEN, :], _NT, preferred_element_type=F32)

    def smpv(u, slot):
        start = key_start(i * blocks + u)
        s = s_sc[slot][...]
        m = jnp.max(s, axis=1, keepdims=True)
        if mode == "win":
            sink = sink_ref[0] * LOG2E
            m = jnp.maximum(m, sink)
        p = jnp.exp2(s - m)
        l = jnp.sum(p, axis=1, keepdims=True)
        if mode == "win":
            l = l + jnp.exp2(sink - m)
        p = p.astype(BF16)
        pv = (jnp.dot(p[:, :nb], v_ref[pl.ds(start, nb), :], preferred_element_type=F32)
              + jnp.dot(p[:, nb:], v_ref[n_lat:n_lat + CTX_LEN, :], preferred_element_type=F32)) / l
        o = jnp.zeros((Q_BLOCK, QUAD), F32)
        for h in range(4):
            o = jnp.where((lane // HEAD_DIM) == h, pv[h * Q_BLOCK:(h + 1) * Q_BLOCK], o)
        o_ref[rows_of(u), :] = o.astype(BF16)

    qk(0, 0)

    def body(w, carry):
        u = 2 * w
        smpv(u, 0); qk(u + 1, 1)
        smpv(u + 1, 1); qk(u + 2, 0)
        return carry

    lax.fori_loop(0, (blocks - 2) // 2, body, 0)
    smpv(blocks - 2, 0); qk(blocks - 1, 1)
    smpv(blocks - 1, 1)


def _quad_attention(qkv, bias, types, sink_col, *, n_lat, nb, mode, q_slab, k_slab, v_slab, n_slabs):
    n_rows = qkv.shape[0]
    n_blocks = n_rows // Q_BLOCK
    blocks = next(b for b in (26, 18, 10, 6, 2) if n_blocks % b == 0)
    n_types = bias.shape[0]
    kern = functools.partial(_quad_kernel, n_lat=n_lat, nb=nb, mode=mode, blocks=blocks)
    in_specs = [
        pl.BlockSpec((blocks * Q_BLOCK, QUAD), lambda j, i, t: (i, q_slab(j))),
        pl.BlockSpec((n_rows, QUAD), lambda j, i, t: (0, k_slab(j)), pipeline_mode=pl.Buffered(1)),
        pl.BlockSpec((n_rows, QUAD), lambda j, i, t: (0, v_slab(j)), pipeline_mode=pl.Buffered(1)),
        pl.BlockSpec((n_types, 1, 4 * Q_BLOCK, nb), lambda j, i, t: (0, j if bias.shape[1] > 1 else 0, 0, 0),
                     pipeline_mode=pl.Buffered(1)),
    ]
    args = [jnp.asarray(types), qkv, qkv, qkv, bias]
    if mode == "win":
        in_specs.append(pl.BlockSpec((1, 4 * Q_BLOCK, 1), lambda j, i, t: (j, 0, 0)))
        args.append(sink_col)
    return pl.pallas_call(
        kern,
        grid_spec=pltpu.PrefetchScalarGridSpec(
            num_scalar_prefetch=1,
            grid=(n_slabs, n_blocks // blocks),
            in_specs=in_specs,
            out_specs=pl.BlockSpec((blocks * Q_BLOCK, QUAD), lambda j, i, t: (i, j)),
            scratch_shapes=[pltpu.VMEM((4 * Q_BLOCK, nb + CTX_LEN), F32)] * 2,
        ),
        out_shape=jax.ShapeDtypeStruct((n_rows, n_slabs * QUAD), BF16),
        compiler_params=_params("arbitrary", "arbitrary"),
        name=mode + "_attn",
    )(*args)


def _rpb_kernel(r_ref, e_ref, keep_ref, o_ref):
    r = r_ref[...] * LOG2E
    acc = jnp.zeros(o_ref.shape, F32)
    for dc in range(e_ref.shape[0]):
        acc = acc + r[:, dc:dc + 1] * e_ref[dc:dc + 1, :]
    o_ref[...] = jnp.where(keep_ref[...] > 0, acc, MASKED)


def _na_bias_tables(rpb, n_lat):
    heads, n_dr, n_dc = rpb.shape
    rows = n_lat // GRID_W
    kh = min(NA_KH, rows)
    qc = np.arange(GRID_W)[:, None]
    kc = np.arange(GRID_W)[None, :]
    c0 = np.clip(qc - NA_KW // 2, 0, GRID_W - NA_KW)
    keep = ((kc >= c0) & (kc < c0 + NA_KW)).reshape(1, -1).astype(np.float32)
    onehot = (np.arange(n_dc)[:, None, None] == (kc - qc + NA_KW - 1)[None]).reshape(n_dc, -1).astype(np.float32)
    tiles = pl.pallas_call(
        _rpb_kernel,
        out_shape=jax.ShapeDtypeStruct((heads * n_dr, GRID_W * GRID_W), F32),
        name="rpb_tiles",
    )(rpb.reshape(heads * n_dr, n_dc).astype(F32), jnp.asarray(onehot), jnp.asarray(keep))
    tiles = tiles.reshape(heads, n_dr, GRID_W, GRID_W)

    type_tables, index_of, types = [], {}, []
    for b in range(rows // 2):
        base = int(np.clip(2 * b - NA_KH // 2, 0, rows - NA_KEY_ROWS))
        r = 2 * b + np.arange(2)[:, None]
        kr = base + np.arange(NA_KEY_ROWS)[None, :]
        r0 = np.clip(r - kh // 2, 0, rows - kh)
        valid = (kr >= r0) & (kr < r0 + kh)
        assert (valid.sum(1) == kh).all()
        drow = np.where(valid, kr - r + (NA_KH - 1), -1)
        key = drow.tobytes()
        if key not in index_of:
            index_of[key] = len(type_tables)
            type_tables.append(drow)
        types.append(index_of[key])
    type_tables.append(np.full((2, NA_KEY_ROWS), -1))
    types += [len(type_tables) - 1] * (CTX_LEN // Q_BLOCK)

    masked_tile = jnp.full((heads, GRID_W, GRID_W), MASKED, F32)
    per_type = []
    for drow in type_tables:
        q_rows = [jnp.concatenate([tiles[:, dr] if dr >= 0 else masked_tile for dr in drow[qr]], axis=-1)
                  for qr in range(2)]
        per_type.append(jnp.concatenate(q_rows, axis=1))
    bias = jnp.stack(per_type)
    n_types, nb = bias.shape[0], bias.shape[-1]
    bias = bias.reshape(n_types, heads // 4, 4 * Q_BLOCK, nb)
    return bias, np.asarray(types, np.int32)


def _na_attention(qkv, rpb, *, n_lat):
    n_quads = qkv.shape[1] // (3 * QUAD)
    bias, types = _na_bias_tables(rpb, n_lat)
    return _quad_attention(qkv, bias, types, None, n_lat=n_lat, nb=NA_KEY_ROWS * GRID_W, mode="na", n_slabs=n_quads,
                           q_slab=lambda j: j, k_slab=lambda j: n_quads + j, v_slab=lambda j: 2 * n_quads + j)


def _window_tables(n_lat, nb):
    n_blk = n_lat // Q_BLOCK
    r = (np.arange(4 * Q_BLOCK) % Q_BLOCK)[:, None]
    c = np.arange(nb)[None, :]
    tables, index_of, types = [], {}, []
    for b in range(n_blk):
        d0 = b * Q_BLOCK - int(np.clip((b - 1) * Q_BLOCK, 0, n_lat - nb))
        if d0 not in index_of:
            index_of[d0] = len(tables)
            tables.append(np.where(np.abs(d0 + r - c) <= WINDOW, 0.0, MASKED).astype(np.float32))
        types.append(index_of[d0])
    tables.append(np.full((4 * Q_BLOCK, nb), MASKED, np.float32))
    types += [len(tables) - 1] * (CTX_LEN // Q_BLOCK)
    return np.stack(tables)[:, None], np.asarray(types, np.int32)


def _window_attention(qkv, sink, *, n_lat):
    n_groups = sink.shape[0] // 4
    nb = 3 * Q_BLOCK
    masks, types = _window_tables(n_lat, nb)
    sink_col = jnp.repeat(sink.astype(F32).reshape(4, n_groups).T, Q_BLOCK, axis=1).reshape(n_groups, 4 * Q_BLOCK, 1)
    return _quad_attention(qkv, jnp.asarray(masks), types, sink_col, n_lat=n_lat, nb=nb, mode="win", n_slabs=n_groups,
                           q_slab=lambda g: g, k_slab=lambda g: n_groups, v_slab=lambda g: n_groups + 1)


def _rope_tables(n_lat):
    half = HEAD_DIM // 2
    freqs = ROPE_THETA ** (-jnp.arange(0, half, 2, dtype=F32) / half)
    pos = jnp.arange(n_lat)
    ang_r = (pos // GRID_W).astype(F32)[:, None] * freqs
    ang_c = (pos % GRID_W).astype(F32)[:, None] * freqs
    cos = jnp.concatenate([jnp.cos(ang_r)] * 2 + [jnp.cos(ang_c)] * 2, axis=-1)
    sin = jnp.concatenate([-jnp.sin(ang_r), jnp.sin(ang_r), -jnp.sin(ang_c), jnp.sin(ang_c)], axis=-1)
    cos = jnp.concatenate([jnp.tile(cos, (1, 2)), jnp.ones((CTX_LEN, LANES), F32)], axis=0)
    sin = jnp.concatenate([jnp.tile(sin, (1, 2)), jnp.zeros((CTX_LEN, LANES), F32)], axis=0)
    return cos, sin


def _scale_q(w, q_cols, scale):
    s = jnp.concatenate([jnp.full((q_cols,), scale, F32), jnp.ones((w.shape[1] - q_cols,), F32)])
    return w * s


def kernel(x, c, ctx, c_ctx, ada_w, ada_b, norm_g, ffn_w13, ffn_w2, a_wqkv, a_wo, a_lambda, a_subln,
           b_wqkv, b_wo, b_rpb, c_wqkv, c_wo, c_sink):
    depth = ada_w.shape[0]
    n_lat = x.shape[1]
    d = x.shape[2]
    xa = jnp.concatenate([x[0], ctx[0]], axis=0)
    mods = _mods(c, c_ctx, ada_w, ada_b).reshape(depth, 2, 6, d)
    cos, sin = _rope_tables(n_lat)
    q_scale = HEAD_DIM ** -0.5
    w13_all = ffn_w13.astype(BF16)
    w2_all = ffn_w2.astype(BF16)

    n_heads_c = c_sink.shape[1]
    kv_c = (c_wqkv.shape[2] - n_heads_c * HEAD_DIM) // (2 * HEAD_DIM)
    grp_c = n_heads_c // kv_c
    perm = (np.arange(kv_c)[None, :, None] * grp_c + np.arange(grp_c)[:, None, None]) * HEAD_DIM + np.arange(HEAD_DIM)[None, None, :]
    perm = perm.reshape(-1)

    for i in range(depth):
        mixer, j = i % 3, i // 3
        g = norm_g[i]
        if mixer == 0:
            width = a_wo.shape[1]
            n_heads = width // LANES
            w = _scale_q(a_wqkv[j], width, q_scale * LOG2E).astype(BF16)
            out_slabs = tuple(range(2 * n_heads)) + tuple(2 * n_heads + 2 * h for h in range(n_heads))
            ones_slabs = tuple(2 * n_heads + 2 * h + 1 for h in range(n_heads))
            qkv = _qkv(xa, mods[i], g[0], w, cos, sin, n_lat=n_lat, rope_cols=2 * width, out_slabs=out_slabs,
                       ones_slabs=ones_slabs)
            lam_init = 0.8 - 0.6 * math.exp(-0.3 * i)
            o = _diff_attention(qkv, a_lambda[j], a_subln[j], n_lat=n_lat, n_heads=n_heads, lam_init=lam_init,
                                with_ctx=i < depth - 1)
            wo = a_wo[j]
        elif mixer == 1:
            width = b_wo.shape[1]
            w = _scale_q(b_wqkv[j], width, q_scale * LOG2E).astype(BF16)
            qkv = _qkv(xa, mods[i], g[0], w, cos, sin, n_lat=n_lat, rope_cols=0)
            o = _na_attention(qkv, b_rpb[j], n_lat=n_lat)
            wo = b_wo[j]
        else:
            width = n_heads_c * HEAD_DIM
            wq = c_wqkv[j]
            w = jnp.concatenate([wq[:, :width][:, perm], wq[:, width:]], axis=1)
            w = _scale_q(w, width, q_scale * LOG2E).astype(BF16)
            qkv = _qkv(xa, mods[i], g[0], w, cos, sin, n_lat=n_lat, rope_cols=width + kv_c * HEAD_DIM)
            o = _window_attention(qkv, c_sink[j], n_lat=n_lat)
            wo = c_wo[j][perm, :]
        out_rows = n_lat if i == depth - 1 else xa.shape[0]
        xa = _post_attn(o, xa, mods[i], g, wo.astype(BF16), w13_all, w2_all, i, n_lat=n_lat, out_rows=out_rows)
    return xa[None]
```

```python
import functools
import math

import numpy as np
import jax
import jax.numpy as jnp
from jax import lax
from jax.experimental import pallas as pl
from jax.experimental.pallas import tpu as pltpu

F32 = jnp.float32
BF16 = jnp.bfloat16

GRID_W = 64
CTX_LEN = 256
NORM_EPS = 1e-6
ROPE_THETA = 10000.0
HEAD_DIM = 64
NA_KH = 8
NA_KW = 16
NA_KEY_ROWS = 10
WINDOW = 128
Q_BLOCK = 128
LANES = 128
MXU_TILE = 256
QUAD = 4 * HEAD_DIM
MASKED = -1e30
LOG2E = math.log2(math.e)
VMEM_LIMIT_BYTES = 56 * 1024 * 1024
_NT = (((1,), (1,)), ((), ()))


def _params(*sem, flags=None):
    return pltpu.CompilerParams(dimension_semantics=sem, vmem_limit_bytes=VMEM_LIMIT_BYTES, flags=flags)


def _row_tile(n_rows):
    for tm in (640, 512, 256, 128):
        if n_rows % tm == 0:
            return tm
    raise ValueError(f"no row tile divides {n_rows}")


def _rms(x, g):
    return x * lax.rsqrt(jnp.mean(x * x, axis=-1, keepdims=True) + NORM_EPS) * g


def _is_ctx(tm, n_lat):
    row = pl.program_id(0) * tm + lax.broadcasted_iota(jnp.int32, (tm, 1), 0)
    return row >= n_lat


def _mod(mod_ref, idx, is_ctx):
    return jnp.where(is_ctx, mod_ref[1, idx:idx + 1, :], mod_ref[0, idx:idx + 1, :])


def _mods_kernel(c_ref, w_ref, b_ref, o_ref):
    s = c_ref[...]
    s = s * jax.nn.sigmoid(s)
    w = w_ref[0]
    b = b_ref[0]
    lat = jnp.sum(w * s[:, 0:1], axis=0, keepdims=True) + b
    cx = jnp.sum(w * s[:, 1:2], axis=0, keepdims=True) + b
    o_ref[0] = jnp.concatenate([lat, cx], axis=0)


def _mods(c, c_ctx, ada_w, ada_b):
    depth, d, n = ada_w.shape
    tn = 1536 if n % 1536 == 0 else 512
    c2 = jnp.stack([c[0], c_ctx], axis=1)
    return pl.pallas_call(
        _mods_kernel,
        grid=(depth, n // tn),
        in_specs=[
            pl.BlockSpec((d, 2), lambda l, j: (0, 0)),
            pl.BlockSpec((1, d, tn), lambda l, j: (l, 0, j)),
            pl.BlockSpec((1, 1, tn), lambda l, j: (l, 0, j)),
        ],
        out_specs=pl.BlockSpec((1, 2, tn), lambda l, j: (l, 0, j)),
        out_shape=jax.ShapeDtypeStruct((depth, 2, n), F32),
        compiler_params=_params("arbitrary", "arbitrary"),
        name="ada_mods",
    )(c2, ada_w, ada_b.reshape(depth, 1, n))


def _qkv_kernel(x_ref, mod_ref, g_ref, w_ref, cos_ref, sin_ref, o_ref, *, tm, n_lat, rope_cols, col_chunk, out_slabs,
                ones_slabs):
    is_ctx = _is_ctx(tm, n_lat)
    h = _rms(x_ref[...], g_ref[...]) * (1.0 + _mod(mod_ref, 1, is_ctx)) + _mod(mod_ref, 0, is_ctx)
    h = h.astype(BF16)
    lane = lax.broadcasted_iota(jnp.int32, (1, LANES), 1)
    first_half = (lane % 32) < 16
    cos = cos_ref[...]
    sin = sin_ref[...]
    for c0 in range(0, w_ref.shape[1], col_chunk):
        acc = jnp.dot(h, w_ref[:, c0:c0 + col_chunk], preferred_element_type=F32)
        for s0 in range(0, col_chunk, LANES):
            a = acc[:, s0:s0 + LANES]
            if c0 + s0 < rope_cols:
                partner = jnp.where(first_half, pltpu.roll(a, LANES - 16, 1), pltpu.roll(a, 16, 1))
                a = a * cos + partner * sin
            dst = out_slabs[(c0 + s0) // LANES] * LANES
            o_ref[:, dst:dst + LANES] = a.astype(BF16)
    for slab in ones_slabs:
        o_ref[:, slab * LANES:(slab + 1) * LANES] = jnp.ones((tm, LANES), BF16)


def _qkv(xa, mod, g, w, cos, sin, *, n_lat, rope_cols, out_slabs=None, ones_slabs=()):
    n_rows, d = xa.shape
    n_in = w.shape[1] // LANES
    out_slabs = tuple(range(n_in)) if out_slabs is None else out_slabs
    ncols = (n_in + len(ones_slabs)) * LANES
    tm = _row_tile(n_rows)
    kern = functools.partial(_qkv_kernel, tm=tm, n_lat=n_lat, rope_cols=rope_cols, col_chunk=512,
                             out_slabs=out_slabs, ones_slabs=ones_slabs)
    return pl.pallas_call(
        kern,
        grid=(n_rows // tm,),
        in_specs=[
            pl.BlockSpec((tm, d), lambda i: (i, 0)),
            pl.BlockSpec((2, 6, d), lambda i: (0, 0, 0)),
            pl.BlockSpec((1, d), lambda i: (0, 0)),
            pl.BlockSpec(w.shape, lambda i: (0, 0)),
            pl.BlockSpec((tm, LANES), lambda i: (i, 0)),
            pl.BlockSpec((tm, LANES), lambda i: (i, 0)),
        ],
        out_specs=pl.BlockSpec((tm, ncols), lambda i: (i, 0)),
        out_shape=jax.ShapeDtypeStruct((n_rows, ncols), BF16),
        compiler_params=_params("arbitrary"),
        name="norm_qkv",
    )(xa, mod, g.reshape(1, d), w, cos, sin)


def _post_attn_kernel(o_ref, x_ref, mod_ref, g_ref, wo_ref, w13_ref, w2_ref, out_ref, *, tm, n_lat, hidden, bounds):
    is_ctx = _is_ctx(tm, n_lat)
    y = jnp.dot(o_ref[...], wo_ref[...], preferred_element_type=F32)
    x = x_ref[...] + _mod(mod_ref, 2, is_ctx) * _rms(y, g_ref[1:2, :])
    h = _rms(x, g_ref[2:3, :]) * (1.0 + _mod(mod_ref, 4, is_ctx)) + _mod(mod_ref, 3, is_ctx)
    h = h.astype(BF16)
    f = None
    for lo, hi in zip(bounds[:-1], bounds[1:]):
        gate = jnp.dot(h, w13_ref[0, :, lo:hi], preferred_element_type=F32)
        up = jnp.dot(h, w13_ref[0, :, hidden + lo:hidden + hi], preferred_element_type=F32)
        act = (gate * jax.nn.sigmoid(gate) * up).astype(BF16)
        part = jnp.dot(act, w2_ref[0, lo:hi, :], preferred_element_type=F32)
        f = part if f is None else f + part
    out_ref[...] = x + _mod(mod_ref, 5, is_ctx) * _rms(f, g_ref[3:4, :])


def _post_attn(o, xa, mod, g, wo, w13_all, w2_all, layer, *, n_lat, out_rows):
    d = xa.shape[1]
    hidden = w2_all.shape[1]
    tm = _row_tile(out_rows)
    mid = -(-hidden // (2 * MXU_TILE)) * MXU_TILE
    bounds = (0, mid, hidden) if 0 < mid < hidden else (0, hidden)
    kern = functools.partial(_post_attn_kernel, tm=tm, n_lat=n_lat, hidden=hidden, bounds=bounds)
    resident = dict(pipeline_mode=pl.Buffered(1))
    return pl.pallas_call(
        kern,
        grid=(out_rows // tm,),
        in_specs=[
            pl.BlockSpec((tm, o.shape[1]), lambda i: (i, 0)),
            pl.BlockSpec((tm, d), lambda i: (i, 0)),
            pl.BlockSpec((2, 6, d), lambda i: (0, 0, 0)),
            pl.BlockSpec(g.shape, lambda i: (0, 0)),
            pl.BlockSpec(wo.shape, lambda i: (0, 0), **resident),
            pl.BlockSpec((1,) + w13_all.shape[1:], lambda i: (layer, 0, 0), **resident),
            pl.BlockSpec((1,) + w2_all.shape[1:], lambda i: (layer, 0, 0), **resident),
        ],
        out_specs=pl.BlockSpec((tm, d), lambda i: (i, 0)),
        out_shape=jax.ShapeDtypeStruct((out_rows, d), F32),
        compiler_params=_params("arbitrary"),
        name="attn_out_ffn",
    )(o, xa, mod, g, wo, w13_all, w2_all)


def _diff_kernel(lam_ref, subln_ref, q_ref, k_ref, v_ref, *rest, tq, tk, tiles, pairs, n_lat, ctx_row0, lam_init):
    o_ref, qq_sc, sctx_sc, s0_sc, s1_sc, m_sc, l_sc, acc_sc = rest[-8:]
    s_sc = (s0_sc, s1_sc)
    lane = lax.broadcasted_iota(jnp.int32, (1, LANES), 1)
    lam = lam_ref[...]
    lam_full = (jnp.exp(jnp.sum(lam[0:1] * lam[1:2], axis=1, keepdims=True))
                - jnp.exp(jnp.sum(lam[2:3] * lam[3:4], axis=1, keepdims=True)) + lam_init)

    for g in range(tiles):
        rows = slice(g * tq, (g + 1) * tq)
        q = q_ref[rows, :]
        zero = jnp.zeros_like(q)
        qq_sc[g] = jnp.concatenate([jnp.where(lane < HEAD_DIM, q, zero), jnp.where(lane >= HEAD_DIM, q, zero)], axis=0)
        m_sc[g] = jnp.full(m_sc.shape[1:], -jnp.inf, F32)
        l_sc[g] = jnp.zeros(l_sc.shape[1:], F32)
        acc_sc[g] = jnp.zeros(acc_sc.shape[1:], F32)

        def scores(row0, size, g=g):
            return lax.dot_general(qq_sc[g], k_ref[pl.ds(row0, size), :], _NT, preferred_element_type=F32)

        def softmax_values(s, row0, size, g=g):
            m_prev = m_sc[g]
            m_next = jnp.maximum(m_prev, jnp.max(s, axis=1)[:, None])
            m_sc[g] = m_next
            p = jnp.exp2(s - jnp.tile(m_next, (1, size // LANES))).astype(BF16)
            alpha = jnp.exp2(m_prev - m_next)
            pv = jnp.dot(p, v_ref[pl.ds(row0, size), :], preferred_element_type=F32)
            acc_sc[g] = alpha * acc_sc[g] + pv[:, :LANES]
            l_sc[g] = alpha * l_sc[g] + pv[:, LANES:]

        if n_lat == 0:
            softmax_values(scores(ctx_row0, CTX_LEN), ctx_row0, CTX_LEN)
        else:
            n_chunks = n_lat // tk

            def pair(t, parity, scores=scores, softmax_values=softmax_values):
                softmax_values(s_sc[parity][...], pl.multiple_of(t * tk, tk), tk)
                s_sc[1 - parity][...] = scores(pl.multiple_of((t + 1) * tk, tk), tk)

            sctx_sc[...] = scores(ctx_row0, CTX_LEN)
            softmax_values(sctx_sc[...], ctx_row0, CTX_LEN)
            s_sc[0][...] = scores(0, tk)

            n_trips = (n_chunks - 1) // pairs

            def body(u, carry, pair=pair):
                for j in range(pairs):
                    pair(u * pairs + j, j % 2)
                return carry

            if n_trips > 0:
                lax.fori_loop(0, n_trips, body, 0)
            for t in range(n_trips * pairs, n_chunks - 1):
                pair(t, t % 2)
            softmax_values(s_sc[(n_chunks - 1) % 2][...], (n_chunks - 1) * tk, tk)

        o2 = acc_sc[g] / l_sc[g]
        o = o2[:tq] - lam_full * o2[tq:]
        o = _rms(o, subln_ref[...]) * (1.0 - lam_init)
        o_ref[rows, :] = o.astype(BF16)


def _diff_call(qkv, lam, subln, *, n_heads, lam_init, tq, tk, tiles, n_lat, q_block0, n_q_blocks, kv_rows, kv_block0,
               out_into=None):
    kern = functools.partial(_diff_kernel, tq=tq, tk=tk, tiles=tiles, pairs=4, n_lat=n_lat, ctx_row0=n_lat,
                             lam_init=lam_init)
    extra_specs, extra_args, aliases = [], [], {}
    if out_into is not None:
        extra_specs, extra_args, aliases = [pl.BlockSpec(memory_space=pl.ANY)], [out_into], {5: 0}
    return pl.pallas_call(
        kern,
        grid=(n_heads, n_q_blocks),
        in_specs=[
            pl.BlockSpec(lam.shape, lambda h, i: (0, 0)),
            pl.BlockSpec((1, LANES), lambda h, i: (0, 0)),
            pl.BlockSpec((tiles * tq, LANES), lambda h, i: (q_block0 + i, h)),
            pl.BlockSpec((kv_rows, LANES), lambda h, i: (kv_block0, n_heads + h)),
            pl.BlockSpec((kv_rows, 2 * LANES), lambda h, i: (kv_block0, n_heads + h)),
        ] + extra_specs,
        out_specs=pl.BlockSpec((tiles * tq, LANES), lambda h, i: (q_block0 + i, h)),
        out_shape=jax.ShapeDtypeStruct((qkv.shape[0], n_heads * LANES), BF16),
        input_output_aliases=aliases,
        scratch_shapes=[
            pltpu.VMEM((tiles, 2 * tq, LANES), BF16),
            pltpu.VMEM((2 * tq, CTX_LEN), F32),
            pltpu.VMEM((2 * tq, tk), F32), pltpu.VMEM((2 * tq, tk), F32),
            pltpu.VMEM((tiles, 2 * tq, LANES), F32),
            pltpu.VMEM((tiles, 2 * tq, LANES), F32),
            pltpu.VMEM((tiles, 2 * tq, LANES), F32),
        ],
        compiler_params=_params("arbitrary", "arbitrary"),
        name="diff_attn",
    )(lam, subln.reshape(1, -1), qkv, qkv, qkv, *extra_args)


def _diff_attention(qkv, lam, subln, *, n_lat, n_heads, lam_init, with_ctx):
    n_rows = qkv.shape[0]
    tq = 512 if n_lat % 1024 == 0 else 256
    tk = 1024 if n_lat % 2048 == 0 else 256
    tiles = 4 if n_lat % (4 * tq) == 0 else 2
    assert n_lat % (tiles * tq) == 0 and n_lat % tk == 0 and n_lat % CTX_LEN == 0
    common = dict(n_heads=n_heads, lam_init=lam_init)
    o_lat = _diff_call(qkv, lam, subln, tq=tq, tk=tk, tiles=tiles, n_lat=n_lat, q_block0=0,
                       n_q_blocks=n_lat // (tiles * tq), kv_rows=n_rows, kv_block0=0, **common)
    if not with_ctx:
        return o_lat
    return _diff_call(qkv, lam, subln, tq=CTX_LEN, tk=CTX_LEN, tiles=1, n_lat=0, q_block0=n_lat // CTX_LEN,
                      n_q_blocks=1, kv_rows=CTX_LEN, kv_block0=n_lat // CTX_LEN, out_into=o_lat, **common)


def _quad_kernel(types_ref, q_ref, k_ref, v_ref, bias_ref, *rest, n_lat, nb, mode, blocks):
    if mode == "win":
        sink_ref, o_ref, s0_sc, s1_sc = rest
    else:
        o_ref, s0_sc, s1_sc = rest
    s_sc = (s0_sc, s1_sc)
    i = pl.program_id(1)
    lane = lax.broadcasted_iota(jnp.int32, (1, QUAD), 1)

    def rows_of(u):
        return pl.ds(pl.multiple_of(u * Q_BLOCK, Q_BLOCK), Q_BLOCK)

    def key_start(blk):
        if mode == "na":
            base = jnp.clip(2 * blk - NA_KH // 2, 0, n_lat // GRID_W - NA_KEY_ROWS)
            return pl.multiple_of(base * GRID_W, GRID_W)
        return pl.multiple_of(jnp.clip((blk - 1) * Q_BLOCK, 0, n_lat - nb), Q_BLOCK)

    def qk(u, slot):
        blk = i * blocks + u
        q = q_ref[rows_of(u), :]
        zero = jnp.zeros_like(q)
        qq = jnp.concatenate([jnp.where((lane // HEAD_DIM) == h, q, zero) for h in range(4)], axis=0)
        sb = lax.dot_general(qq, k_ref[pl.ds(key_start(blk), nb), :], _NT, preferred_element_type=F32)
        s_sc[slot][:, :nb] = sb + bias_ref[types_ref[blk], 0]
        s_sc[slot][:, nb:] = lax.dot_general(qq, k_ref[n_lat:n_lat + CTX_LEN, :], _NT, preferred_element_type=F32)

    def smpv(u, slot):
        start = key_start(i * blocks + u)
        s = s_sc[slot][...]
        m = jnp.max(s, axis=1, keepdims=True)
        if mode == "win":
            sink = sink_ref[0] * LOG2E
            m = jnp.maximum(m, sink)
        p = jnp.exp2(s - m)
        l = jnp.sum(p, axis=1, keepdims=True)
        if mode == "win":
            l = l + jnp.exp2(sink - m)
        p = p.astype(BF16)
        pv = (jnp.dot(p[:, :nb], v_ref[pl.ds(start, nb), :], preferred_element_type=F32)
              + jnp.dot(p[:, nb:], v_ref[n_lat:n_lat + CTX_LEN, :], preferred_element_type=F32)) / l
        o = jnp.zeros((Q_BLOCK, QUAD), F32)
        for h in range(4):
            o = jnp.where((lane // HEAD_DIM) == h, pv[h * Q_BLOCK:(h + 1) * Q_BLOCK], o)
        o_ref[rows_of(u), :] = o.astype(BF16)

    qk(0, 0)

    def body(w, carry):
        u = 2 * w
        smpv(u, 0); qk(u + 1, 1)
        smpv(u + 1, 1); qk(u + 2, 0)
        return carry

    lax.fori_loop(0, (blocks - 2) // 2, body, 0)
    smpv(blocks - 2, 0); qk(blocks - 1, 1)
    smpv(blocks - 1, 1)


def _quad_attention(qkv, bias, types, sink_col, *, n_lat, nb, mode, q_slab, k_slab, v_slab, n_slabs):
    n_rows = qkv.shape[0]
    n_blocks = n_rows // Q_BLOCK
    blocks = next(b for b in (26, 18, 10, 6, 2) if n_blocks % b == 0)
    n_types = bias.shape[0]
    kern = functools.partial(_quad_kernel, n_lat=n_lat, nb=nb, mode=mode, blocks=blocks)
    in_specs = [
        pl.BlockSpec((blocks * Q_BLOCK, QUAD), lambda j, i, t: (i, q_slab(j))),
        pl.BlockSpec((n_rows, QUAD), lambda j, i, t: (0, k_slab(j)), pipeline_mode=pl.Buffered(1)),
        pl.BlockSpec((n_rows, QUAD), lambda j, i, t: (0, v_slab(j)), pipeline_mode=pl.Buffered(1)),
        pl.BlockSpec((n_types, 1, 4 * Q_BLOCK, nb), lambda j, i, t: (0, j if bias.shape[1] > 1 else 0, 0, 0),
                     pipeline_mode=pl.Buffered(1)),
    ]
    args = [jnp.asarray(types), qkv, qkv, qkv, bias]
    if mode == "win":
        in_specs.append(pl.BlockSpec((1, 4 * Q_BLOCK, 1), lambda j, i, t: (j, 0, 0)))
        args.append(sink_col)
    return pl.pallas_call(
        kern,
        grid_spec=pltpu.PrefetchScalarGridSpec(
            num_scalar_prefetch=1,
            grid=(n_slabs, n_blocks // blocks),
            in_specs=in_specs,
            out_specs=pl.BlockSpec((blocks * Q_BLOCK, QUAD), lambda j, i, t: (i, j)),
            scratch_shapes=[pltpu.VMEM((4 * Q_BLOCK, nb + CTX_LEN), F32)] * 2,
        ),
        out_shape=jax.ShapeDtypeStruct((n_rows, n_slabs * QUAD), BF16),
        compiler_params=_params("arbitrary", "arbitrary"),
        name=mode + "_attn",
    )(*args)


def _rpb_kernel(r_ref, e_ref, keep_ref, o_ref):
    r = r_ref[...] * LOG2E
    acc = jnp.zeros(o_ref.shape, F32)
    for dc in range(e_ref.shape[0]):
        acc = acc + r[:, dc:dc + 1] * e_ref[dc:dc + 1, :]
    o_ref[...] = jnp.where(keep_ref[...] > 0, acc, MASKED)


def _na_bias_tables(rpb, n_lat):
    heads, n_dr, n_dc = rpb.shape
    rows = n_lat // GRID_W
    kh = min(NA_KH, rows)
    qc = np.arange(GRID_W)[:, None]
    kc = np.arange(GRID_W)[None, :]
    c0 = np.clip(qc - NA_KW // 2, 0, GRID_W - NA_KW)
    keep = ((kc >= c0) & (kc < c0 + NA_KW)).reshape(1, -1).astype(np.float32)
    onehot = (np.arange(n_dc)[:, None, None] == (kc - qc + NA_KW - 1)[None]).reshape(n_dc, -1).astype(np.float32)
    tiles = pl.pallas_call(
        _rpb_kernel,
        out_shape=jax.ShapeDtypeStruct((heads * n_dr, GRID_W * GRID_W), F32),
        name="rpb_tiles",
    )(rpb.reshape(heads * n_dr, n_dc).astype(F32), jnp.asarray(onehot), jnp.asarray(keep))
    tiles = tiles.reshape(heads, n_dr, GRID_W, GRID_W)

    type_tables, index_of, types = [], {}, []
    for b in range(rows // 2):
        base = int(np.clip(2 * b - NA_KH // 2, 0, rows - NA_KEY_ROWS))
        r = 2 * b + np.arange(2)[:, None]
        kr = base + np.arange(NA_KEY_ROWS)[None, :]
        r0 = np.clip(r - kh // 2, 0, rows - kh)
        valid = (kr >= r0) & (kr < r0 + kh)
        assert (valid.sum(1) == kh).all()
        drow = np.where(valid, kr - r + (NA_KH - 1), -1)
        key = drow.tobytes()
        if key not in index_of:
            index_of[key] = len(type_tables)
            type_tables.append(drow)
        types.append(index_of[key])
    type_tables.append(np.full((2, NA_KEY_ROWS), -1))
    types += [len(type_tables) - 1] * (CTX_LEN // Q_BLOCK)

    masked_tile = jnp.full((heads, GRID_W, GRID_W), MASKED, F32)
    per_type = []
    for drow in type_tables:
        q_rows = [jnp.concatenate([tiles[:, dr] if dr >= 0 else masked_tile for dr in drow[qr]], axis=-1)
                  for qr in range(2)]
        per_type.append(jnp.concatenate(q_rows, axis=1))
    bias = jnp.stack(per_type)
    n_types, nb = bias.shape[0], bias.shape[-1]
    bias = bias.reshape(n_types, heads // 4, 4 * Q_BLOCK, nb)
    return bias, np.asarray(types, np.int32)


def _na_attention(qkv, rpb, *, n_lat):
    n_quads = qkv.shape[1] // (3 * QUAD)
    bias, types = _na_bias_tables(rpb, n_lat)
    return _quad_attention(qkv, bias, types, None, n_lat=n_lat, nb=NA_KEY_ROWS * GRID_W, mode="na", n_slabs=n_quads,
                           q_slab=lambda j: j, k_slab=lambda j: n_quads + j, v_slab=lambda j: 2 * n_quads + j)


def _window_tables(n_lat, nb):
    n_blk = n_lat // Q_BLOCK
    r = (np.arange(4 * Q_BLOCK) % Q_BLOCK)[:, None]
    c = np.arange(nb)[None, :]
    tables, index_of, types = [], {}, []
    for b in range(n_blk):
        d0 = b * Q_BLOCK - int(np.clip((b - 1) * Q_BLOCK, 0, n_lat - nb))
        if d0 not in index_of:
            index_of[d0] = len(tables)
            tables.append(np.where(np.abs(d0 + r - c) <= WINDOW, 0.0, MASKED).astype(np.float32))
        types.append(index_of[d0])
    tables.append(np.full((4 * Q_BLOCK, nb), MASKED, np.float32))
    types += [len(tables) - 1] * (CTX_LEN // Q_BLOCK)
    return np.stack(tables)[:, None], np.asarray(types, np.int32)


def _window_attention(qkv, sink, *, n_lat):
    n_groups = sink.shape[0] // 4
    nb = 3 * Q_BLOCK
    masks, types = _window_tables(n_lat, nb)
    sink_col = jnp.repeat(sink.astype(F32).reshape(4, n_groups).T, Q_BLOCK, axis=1).reshape(n_groups, 4 * Q_BLOCK, 1)
    return _quad_attention(qkv, jnp.asarray(masks), types, sink_col, n_lat=n_lat, nb=nb, mode="win", n_slabs=n_groups,
                           q_slab=lambda g: g, k_slab=lambda g: n_groups, v_slab=lambda g: n_groups + 1)


def _rope_tables(n_lat):
    half = HEAD_DIM // 2
    freqs = ROPE_THETA ** (-jnp.arange(0, half, 2, dtype=F32) / half)
    pos = jnp.arange(n_lat)
    ang_r = (pos // GRID_W).astype(F32)[:, None] * freqs
    ang_c = (pos % GRID_W).astype(F32)[:, None] * freqs
    cos = jnp.concatenate([jnp.cos(ang_r)] * 2 + [jnp.cos(ang_c)] * 2, axis=-1)
    sin = jnp.concatenate([-jnp.sin(ang_r), jnp.sin(ang_r), -jnp.sin(ang_c), jnp.sin(ang_c)], axis=-1)
    cos = jnp.concatenate([jnp.tile(cos, (1, 2)), jnp.ones((CTX_LEN, LANES), F32)], axis=0)
    sin = jnp.concatenate([jnp.tile(sin, (1, 2)), jnp.zeros((CTX_LEN, LANES), F32)], axis=0)
    return cos, sin


def _scale_q(w, q_cols, scale):
    s = jnp.concatenate([jnp.full((q_cols,), scale, F32), jnp.ones((w.shape[1] - q_cols,), F32)])
    return w * s


def kernel(x, c, ctx, c_ctx, ada_w, ada_b, norm_g, ffn_w13, ffn_w2, a_wqkv, a_wo, a_lambda, a_subln,
           b_wqkv, b_wo, b_rpb, c_wqkv, c_wo, c_sink):
    depth = ada_w.shape[0]
    n_lat = x.shape[1]
    d = x.shape[2]
    xa = jnp.concatenate([x[0], ctx[0]], axis=0)
    mods = _mods(c, c_ctx, ada_w, ada_b).reshape(depth, 2, 6, d)
    cos, sin = _rope_tables(n_lat)
    q_scale = HEAD_DIM ** -0.5
    w13_all = ffn_w13.astype(BF16)
    w2_all = ffn_w2.astype(BF16)

    n_heads_c = c_sink.shape[1]
    kv_c = (c_wqkv.shape[2] - n_heads_c * HEAD_DIM) // (2 * HEAD_DIM)
    grp_c = n_heads_c // kv_c
    perm = (np.arange(kv_c)[None, :, None] * grp_c + np.arange(grp_c)[:, None, None]) * HEAD_DIM + np.arange(HEAD_DIM)[None, None, :]
    perm = perm.reshape(-1)

    for i in range(depth):
        mixer, j = i % 3, i // 3
        g = norm_g[i]
        if mixer == 0:
            width = a_wo.shape[1]
            n_heads = width // LANES
            w = _scale_q(a_wqkv[j], width, q_scale * LOG2E).astype(BF16)
            out_slabs = tuple(range(2 * n_heads)) + tuple(2 * n_heads + 2 * h for h in range(n_heads))
            ones_slabs = tuple(2 * n_heads + 2 * h + 1 for h in range(n_heads))
            qkv = _qkv(xa, mods[i], g[0], w, cos, sin, n_lat=n_lat, rope_cols=2 * width, out_slabs=out_slabs,
                       ones_slabs=ones_slabs)
            lam_init = 0.8 - 0.6 * math.exp(-0.3 * i)
            o = _diff_attention(qkv, a_lambda[j], a_subln[j], n_lat=n_lat, n_heads=n_heads, lam_init=lam_init,
                                with_ctx=i < depth - 1)
            wo = a_wo[j]
        elif mixer == 1:
            width = b_wo.shape[1]
            w = _scale_q(b_wqkv[j], width, q_scale * LOG2E).astype(BF16)
            qkv = _qkv(xa, mods[i], g[0], w, cos, sin, n_lat=n_lat, rope_cols=0)
            o = _na_attention(qkv, b_rpb[j], n_lat=n_lat)
            wo = b_wo[j]
        else:
            width = n_heads_c * HEAD_DIM
            wq = c_wqkv[j]
            w = jnp.concatenate([wq[:, :width][:, perm], wq[:, width:]], axis=1)
            w = _scale_q(w, width, q_scale * LOG2E).astype(BF16)
            qkv = _qkv(xa, mods[i], g[0], w, cos, sin, n_lat=n_lat, rope_cols=width + kv_c * HEAD_DIM)
            o = _window_attention(qkv, c_sink[j], n_lat=n_lat)
            wo = c_wo[j][perm, :]
        out_rows = n_lat if i == depth - 1 else xa.shape[0]
        xa = _post_attn(o, xa, mods[i], g, wo.astype(BF16), w13_all, w2_all, i, n_lat=n_lat, out_rows=out_rows)
    return xa[None]
```

```python
import functools
import math

import numpy as np
import jax
import jax.numpy as jnp
from jax import lax
from jax.experimental import pallas as pl
from jax.experimental.pallas import tpu as pltpu

F32 = jnp.float32
BF16 = jnp.bfloat16

GRID_W = 64
CTX_LEN = 256
NORM_EPS = 1e-6
ROPE_THETA = 10000.0
HEAD_DIM = 64
NA_KH = 8
NA_KW = 16
NA_KEY_ROWS = 10
WINDOW = 128
Q_BLOCK = 128
LANES = 128
MXU_TILE = 256
QUAD = 4 * HEAD_DIM
MASKED = -1e30
LOG2E = math.log2(math.e)
VMEM_LIMIT_BYTES = 56 * 1024 * 1024
_NT = (((1,), (1,)), ((), ()))


def _params(*sem, flags=None):
    return pltpu.CompilerParams(dimension_semantics=sem, vmem_limit_bytes=VMEM_LIMIT_BYTES, flags=flags)


def _row_tile(n_rows):
    for tm in (640, 512, 256, 128):
        if n_rows % tm == 0:
            return tm
    raise ValueError(f"no row tile divides {n_rows}")


def _rms(x, g):
    return x * lax.rsqrt(jnp.mean(x * x, axis=-1, keepdims=True) + NORM_EPS) * g


def _is_ctx(tm, n_lat):
    row = pl.program_id(0) * tm + lax.broadcasted_iota(jnp.int32, (tm, 1), 0)
    return row >= n_lat


def _mod(mod_ref, idx, is_ctx):
    return jnp.where(is_ctx, mod_ref[1, idx:idx + 1, :], mod_ref[0, idx:idx + 1, :])


def _mods_kernel(c_ref, w_ref, b_ref, o_ref):
    s = c_ref[...]
    s = s * jax.nn.sigmoid(s)
    w = w_ref[0]
    b = b_ref[0]
    lat = jnp.sum(w * s[:, 0:1], axis=0, keepdims=True) + b
    cx = jnp.sum(w * s[:, 1:2], axis=0, keepdims=True) + b
    o_ref[0] = jnp.concatenate([lat, cx], axis=0)


def _mods(c, c_ctx, ada_w, ada_b):
    depth, d, n = ada_w.shape
    tn = 1536 if n % 1536 == 0 else 512
    c2 = jnp.stack([c[0], c_ctx], axis=1)
    return pl.pallas_call(
        _mods_kernel,
        grid=(depth, n // tn),
        in_specs=[
            pl.BlockSpec((d, 2), lambda l, j: (0, 0)),
            pl.BlockSpec((1, d, tn), lambda l, j: (l, 0, j)),
            pl.BlockSpec((1, 1, tn), lambda l, j: (l, 0, j)),
        ],
        out_specs=pl.BlockSpec((1, 2, tn), lambda l, j: (l, 0, j)),
        out_shape=jax.ShapeDtypeStruct((depth, 2, n), F32),
        compiler_params=_params("arbitrary", "arbitrary"),
        name="ada_mods",
    )(c2, ada_w, ada_b.reshape(depth, 1, n))


def _qkv_kernel(x_ref, mod_ref, g_ref, w_ref, cos_ref, sin_ref, o_ref, *, tm, n_lat, rope_cols, col_chunk, out_slabs,
                ones_slabs):
    is_ctx = _is_ctx(tm, n_lat)
    h = _rms(x_ref[...], g_ref[...]) * (1.0 + _mod(mod_ref, 1, is_ctx)) + _mod(mod_ref, 0, is_ctx)
    h = h.astype(BF16)
    lane = lax.broadcasted_iota(jnp.int32, (1, LANES), 1)
    first_half = (lane % 32) < 16
    cos = cos_ref[...]
    sin = sin_ref[...]
    for c0 in range(0, w_ref.shape[1], col_chunk):
        acc = jnp.dot(h, w_ref[:, c0:c0 + col_chunk], preferred_element_type=F32)
        for s0 in range(0, col_chunk, LANES):
            a = acc[:, s0:s0 + LANES]
            if c0 + s0 < rope_cols:
                partner = jnp.where(first_half, pltpu.roll(a, LANES - 16, 1), pltpu.roll(a, 16, 1))
                a = a * cos + partner * sin
            dst = out_slabs[(c0 + s0) // LANES] * LANES
            o_ref[:, dst:dst + LANES] = a.astype(BF16)
    for slab in ones_slabs:
        o_ref[:, slab * LANES:(slab + 1) * LANES] = jnp.ones((tm, LANES), BF16)


def _qkv(xa, mod, g, w, cos, sin, *, n_lat, rope_cols, out_slabs=None, ones_slabs=()):
    n_rows, d = xa.shape
    n_in = w.shape[1] // LANES
    out_slabs = tuple(range(n_in)) if out_slabs is None else out_slabs
    ncols = (n_in + len(ones_slabs)) * LANES
    tm = _row_tile(n_rows)
    kern = functools.partial(_qkv_kernel, tm=tm, n_lat=n_lat, rope_cols=rope_cols, col_chunk=512,
                             out_slabs=out_slabs, ones_slabs=ones_slabs)
    return pl.pallas_call(
        kern,
        grid=(n_rows // tm,),
        in_specs=[
            pl.BlockSpec((tm, d), lambda i: (i, 0)),
            pl.BlockSpec((2, 6, d), lambda i: (0, 0, 0)),
            pl.BlockSpec((1, d), lambda i: (0, 0)),
            pl.BlockSpec(w.shape, lambda i: (0, 0)),
            pl.BlockSpec((tm, LANES), lambda i: (i, 0)),
            pl.BlockSpec((tm, LANES), lambda i: (i, 0)),
        ],
        out_specs=pl.BlockSpec((tm, ncols), lambda i: (i, 0)),
        out_shape=jax.ShapeDtypeStruct((n_rows, ncols), BF16),
        compiler_params=_params("arbitrary"),
        name="norm_qkv",
    )(xa, mod, g.reshape(1, d), w, cos, sin)


def _post_attn_kernel(o_ref, x_ref, mod_ref, g_ref, wo_ref, w13_ref, w2_ref, out_ref, *, tm, n_lat, hidden, bounds):
    is_ctx = _is_ctx(tm, n_lat)
    y = jnp.dot(o_ref[...], wo_ref[...], preferred_element_type=F32)
    x = x_ref[...] + _mod(mod_ref, 2, is_ctx) * _rms(y, g_ref[1:2, :])
    h = _rms(x, g_ref[2:3, :]) * (1.0 + _mod(mod_ref, 4, is_ctx)) + _mod(mod_ref, 3, is_ctx)
    h = h.astype(BF16)
    f = None
    for lo, hi in zip(bounds[:-1], bounds[1:]):
        gate = jnp.dot(h, w13_ref[0, :, lo:hi], preferred_element_type=F32)
        up = jnp.dot(h, w13_ref[0, :, hidden + lo:hidden + hi], preferred_element_type=F32)
        act = (gate * jax.nn.sigmoid(gate) * up).astype(BF16)
        part = jnp.dot(act, w2_ref[0, lo:hi, :], preferred_element_type=F32)
        f = part if f is None else f + part
    out_ref[...] = x + _mod(mod_ref, 5, is_ctx) * _rms(f, g_ref[3:4, :])


def _post_attn(o, xa, mod, g, wo, w13_all, w2_all, layer, *, n_lat, out_rows):
    d = xa.shape[1]
    hidden = w2_all.shape[1]
    tm = _row_tile(out_rows)
    mid = -(-hidden // (2 * MXU_TILE)) * MXU_TILE
    bounds = (0, mid, hidden) if 0 < mid < hidden else (0, hidden)
    kern = functools.partial(_post_attn_kernel, tm=tm, n_lat=n_lat, hidden=hidden, bounds=bounds)
    resident = dict(pipeline_mode=pl.Buffered(1))
    return pl.pallas_call(
        kern,
        grid=(out_rows // tm,),
        in_specs=[
            pl.BlockSpec((tm, o.shape[1]), lambda i: (i, 0)),
            pl.BlockSpec((tm, d), lambda i: (i, 0)),
            pl.BlockSpec((2, 6, d), lambda i: (0, 0, 0)),
            pl.BlockSpec(g.shape, lambda i: (0, 0)),
            pl.BlockSpec(wo.shape, lambda i: (0, 0), **resident),
            pl.BlockSpec((1,) + w13_all.shape[1:], lambda i: (layer, 0, 0), **resident),
            pl.BlockSpec((1,) + w2_all.shape[1:], lambda i: (layer, 0, 0), **resident),
        ],
        out_specs=pl.BlockSpec((tm, d), lambda i: (i, 0)),
        out_shape=jax.ShapeDtypeStruct((out_rows, d), F32),
        compiler_params=_params("arbitrary"),
        name="attn_out_ffn",
    )(o, xa, mod, g, wo, w13_all, w2_all)


def _diff_kernel(lam_ref, subln_ref, q_ref, k_ref, v_ref, o_ref, qq_sc, sctx_sc, s0_sc, s1_sc, m_sc, l_sc, acc_sc, *, tq,
                 tk, tiles, pairs, n_lat, ctx_row0, lam_init):
    s_sc = (s0_sc, s1_sc)
    lane = lax.broadcasted_iota(jnp.int32, (1, LANES), 1)
    lam = lam_ref[...]
    lam_full = (jnp.exp(jnp.sum(lam[0:1] * lam[1:2], axis=1, keepdims=True))
                - jnp.exp(jnp.sum(lam[2:3] * lam[3:4], axis=1, keepdims=True)) + lam_init)

    for g in range(tiles):
        rows = slice(g * tq, (g + 1) * tq)
        q = q_ref[rows, :]
        zero = jnp.zeros_like(q)
        qq_sc[g] = jnp.concatenate([jnp.where(lane < HEAD_DIM, q, zero), jnp.where(lane >= HEAD_DIM, q, zero)], axis=0)
        m_sc[g] = jnp.full(m_sc.shape[1:], -jnp.inf, F32)
        l_sc[g] = jnp.zeros(l_sc.shape[1:], F32)
        acc_sc[g] = jnp.zeros(acc_sc.shape[1:], F32)

        def scores(row0, size, g=g):
            return lax.dot_general(qq_sc[g], k_ref[pl.ds(row0, size), :], _NT, preferred_element_type=F32)

        def softmax_values(s, row0, size, g=g):
            m_prev = m_sc[g]
            m_next = jnp.maximum(m_prev, jnp.max(s, axis=1)[:, None])
            m_sc[g] = m_next
            p = jnp.exp2(s - jnp.tile(m_next, (1, size // LANES))).astype(BF16)
            alpha = jnp.exp2(m_prev - m_next)
            pv = jnp.dot(p, v_ref[pl.ds(row0, size), :], preferred_element_type=F32)
            acc_sc[g] = alpha * acc_sc[g] + pv[:, :LANES]
            l_sc[g] = alpha * l_sc[g] + pv[:, LANES:]

        if n_lat == 0:
            softmax_values(scores(ctx_row0, CTX_LEN), ctx_row0, CTX_LEN)
        else:
            n_chunks = n_lat // tk

            def pair(t, parity, scores=scores, softmax_values=softmax_values):
                softmax_values(s_sc[parity][...], pl.multiple_of(t * tk, tk), tk)
                s_sc[1 - parity][...] = scores(pl.multiple_of((t + 1) * tk, tk), tk)

            sctx_sc[...] = scores(ctx_row0, CTX_LEN)
            softmax_values(sctx_sc[...], ctx_row0, CTX_LEN)
            s_sc[0][...] = scores(0, tk)

            n_trips = (n_chunks - 1) // pairs

            def body(u, carry, pair=pair):
                for j in range(pairs):
                    pair(u * pairs + j, j % 2)
                return carry

            if n_trips > 0:
                lax.fori_loop(0, n_trips, body, 0)
            for t in range(n_trips * pairs, n_chunks - 1):
                pair(t, t % 2)
            softmax_values(s_sc[(n_chunks - 1) % 2][...], (n_chunks - 1) * tk, tk)

        o2 = acc_sc[g] / l_sc[g]
        o = o2[:tq] - lam_full * o2[tq:]
        o = _rms(o, subln_ref[...]) * (1.0 - lam_init)
        o_ref[rows, :] = o.astype(BF16)


def _diff_call(qkv, lam, subln, *, n_heads, lam_init, tq, tk, tiles, n_lat, q_block0, n_q_blocks, kv_rows, kv_block0):
    kern = functools.partial(_diff_kernel, tq=tq, tk=tk, tiles=tiles, pairs=4, n_lat=n_lat, ctx_row0=n_lat,
                             lam_init=lam_init)
    return pl.pallas_call(
        kern,
        grid=(n_heads, n_q_blocks),
        in_specs=[
            pl.BlockSpec(lam.shape, lambda h, i: (0, 0)),
            pl.BlockSpec((1, LANES), lambda h, i: (0, 0)),
            pl.BlockSpec((tiles * tq, LANES), lambda h, i: (q_block0 + i, h)),
            pl.BlockSpec((kv_rows, LANES), lambda h, i: (kv_block0, n_heads + h), pipeline_mode=pl.Buffered(1)),
            pl.BlockSpec((kv_rows, 2 * LANES), lambda h, i: (kv_block0, n_heads + h), pipeline_mode=pl.Buffered(1)),
        ],
        out_specs=pl.BlockSpec((tiles * tq, LANES), lambda h, i: (i, h)),
        out_shape=jax.ShapeDtypeStruct((n_q_blocks * tiles * tq, n_heads * LANES), BF16),
        scratch_shapes=[
            pltpu.VMEM((tiles, 2 * tq, LANES), BF16),
            pltpu.VMEM((2 * tq, CTX_LEN), F32),
            pltpu.VMEM((2 * tq, tk), F32), pltpu.VMEM((2 * tq, tk), F32),
            pltpu.VMEM((tiles, 2 * tq, LANES), F32),
            pltpu.VMEM((tiles, 2 * tq, LANES), F32),
            pltpu.VMEM((tiles, 2 * tq, LANES), F32),
        ],
        compiler_params=_params("arbitrary", "arbitrary"),
        name="diff_attn",
    )(lam, subln.reshape(1, -1), qkv, qkv, qkv)


def _diff_attention(qkv, lam, subln, *, n_lat, n_heads, lam_init, with_ctx):
    n_rows = qkv.shape[0]
    tq = 512 if n_lat % 1024 == 0 else 256
    tk = 1024 if n_lat % 2048 == 0 else 256
    tiles = 2
    assert n_lat % (tiles * tq) == 0 and n_lat % tk == 0 and n_lat % CTX_LEN == 0
    common = dict(n_heads=n_heads, lam_init=lam_init)
    o_lat = _diff_call(qkv, lam, subln, tq=tq, tk=tk, tiles=tiles, n_lat=n_lat, q_block0=0,
                       n_q_blocks=n_lat // (tiles * tq), kv_rows=n_rows, kv_block0=0, **common)
    if not with_ctx:
        return o_lat
    o_ctx = _diff_call(qkv, lam, subln, tq=CTX_LEN, tk=CTX_LEN, tiles=1, n_lat=0, q_block0=n_lat // CTX_LEN,
                       n_q_blocks=1, kv_rows=CTX_LEN, kv_block0=n_lat // CTX_LEN, **common)
    return jnp.concatenate([o_lat, o_ctx], axis=0)


def _quad_kernel(types_ref, q_ref, k_ref, v_ref, bias_ref, *rest, n_lat, nb, mode, blocks):
    if mode == "win":
        sink_ref, o_ref, s0_sc, s1_sc = rest
    else:
        o_ref, s0_sc, s1_sc = rest
    s_sc = (s0_sc, s1_sc)
    i = pl.program_id(1)
    lane = lax.broadcasted_iota(jnp.int32, (1, QUAD), 1)

    def rows_of(u):
        return pl.ds(pl.multiple_of(u * Q_BLOCK, Q_BLOCK), Q_BLOCK)

    def key_start(blk):
        if mode == "na":
            base = jnp.clip(2 * blk - NA_KH // 2, 0, n_lat // GRID_W - NA_KEY_ROWS)
            return pl.multiple_of(base * GRID_W, GRID_W)
        return pl.multiple_of(jnp.clip((blk - 1) * Q_BLOCK, 0, n_lat - nb), Q_BLOCK)

    def qk(u, slot):
        blk = i * blocks + u
        q = q_ref[rows_of(u), :]
        zero = jnp.zeros_like(q)
        qq = jnp.concatenate([jnp.where((lane // HEAD_DIM) == h, q, zero) for h in range(4)], axis=0)
        sb = lax.dot_general(qq, k_ref[pl.ds(key_start(blk), nb), :], _NT, preferred_element_type=F32)
        s_sc[slot][:, :nb] = sb + bias_ref[types_ref[blk], 0]
        s_sc[slot][:, nb:] = lax.dot_general(qq, k_ref[n_lat:n_lat + CTX_LEN, :], _NT, preferred_element_type=F32)

    def smpv(u, slot):
        start = key_start(i * blocks + u)
        s = s_sc[slot][...]
        m = jnp.max(s, axis=1, keepdims=True)
        if mode == "win":
            sink = sink_ref[0] * LOG2E
            m = jnp.maximum(m, sink)
        p = jnp.exp2(s - m)
        l = jnp.sum(p, axis=1, keepdims=True)
        if mode == "win":
            l = l + jnp.exp2(sink - m)
        p = p.astype(BF16)
        pv = (jnp.dot(p[:, :nb], v_ref[pl.ds(start, nb), :], preferred_element_type=F32)
              + jnp.dot(p[:, nb:], v_ref[n_lat:n_lat + CTX_LEN, :], preferred_element_type=F32)) / l
        o = jnp.zeros((Q_BLOCK, QUAD), F32)
        for h in range(4):
            o = jnp.where((lane // HEAD_DIM) == h, pv[h * Q_BLOCK:(h + 1) * Q_BLOCK], o)
        o_ref[rows_of(u), :] = o.astype(BF16)

    qk(0, 0)

    def body(w, carry):
        u = 2 * w
        smpv(u, 0); qk(u + 1, 1)
        smpv(u + 1, 1); qk(u + 2, 0)
        return carry

    lax.fori_loop(0, (blocks - 2) // 2, body, 0)
    smpv(blocks - 2, 0); qk(blocks - 1, 1)
    smpv(blocks - 1, 1)


def _quad_attention(qkv, bias, types, sink_col, *, n_lat, nb, mode, q_slab, k_slab, v_slab, n_slabs):
    n_rows = qkv.shape[0]
    n_blocks = n_rows // Q_BLOCK
    blocks = next(b for b in (26, 18, 10, 6, 2) if n_blocks % b == 0)
    n_types = bias.shape[0]
    kern = functools.partial(_quad_kernel, n_lat=n_lat, nb=nb, mode=mode, blocks=blocks)
    in_specs = [
        pl.BlockSpec((blocks * Q_BLOCK, QUAD), lambda j, i, t: (i, q_slab(j))),
        pl.BlockSpec((n_rows, QUAD), lambda j, i, t: (0, k_slab(j)), pipeline_mode=pl.Buffered(1)),
        pl.BlockSpec((n_rows, QUAD), lambda j, i, t: (0, v_slab(j)), pipeline_mode=pl.Buffered(1)),
        pl.BlockSpec((n_types, 1, 4 * Q_BLOCK, nb), lambda j, i, t: (0, j if bias.shape[1] > 1 else 0, 0, 0),
                     pipeline_mode=pl.Buffered(1)),
    ]
    args = [jnp.asarray(types), qkv, qkv, qkv, bias]
    if mode == "win":
        in_specs.append(pl.BlockSpec((1, 4 * Q_BLOCK, 1), lambda j, i, t: (j, 0, 0)))
        args.append(sink_col)
    return pl.pallas_call(
        kern,
        grid_spec=pltpu.PrefetchScalarGridSpec(
            num_scalar_prefetch=1,
            grid=(n_slabs, n_blocks // blocks),
            in_specs=in_specs,
            out_specs=pl.BlockSpec((blocks * Q_BLOCK, QUAD), lambda j, i, t: (i, j)),
            scratch_shapes=[pltpu.VMEM((4 * Q_BLOCK, nb + CTX_LEN), F32)] * 2,
        ),
        out_shape=jax.ShapeDtypeStruct((n_rows, n_slabs * QUAD), BF16),
        compiler_params=_params("arbitrary", "arbitrary"),
        name=mode + "_attn",
    )(*args)


def _rpb_kernel(r_ref, e_ref, keep_ref, o_ref):
    r = r_ref[...] * LOG2E
    acc = jnp.zeros(o_ref.shape, F32)
    for dc in range(e_ref.shape[0]):
        acc = acc + r[:, dc:dc + 1] * e_ref[dc:dc + 1, :]
    o_ref[...] = jnp.where(keep_ref[...] > 0, acc, MASKED)


def _na_bias_tables(rpb, n_lat):
    heads, n_dr, n_dc = rpb.shape
    rows = n_lat // GRID_W
    kh = min(NA_KH, rows)
    qc = np.arange(GRID_W)[:, None]
    kc = np.arange(GRID_W)[None, :]
    c0 = np.clip(qc - NA_KW // 2, 0, GRID_W - NA_KW)
    keep = ((kc >= c0) & (kc < c0 + NA_KW)).reshape(1, -1).astype(np.float32)
    onehot = (np.arange(n_dc)[:, None, None] == (kc - qc + NA_KW - 1)[None]).reshape(n_dc, -1).astype(np.float32)
    tiles = pl.pallas_call(
        _rpb_kernel,
        out_shape=jax.ShapeDtypeStruct((heads * n_dr, GRID_W * GRID_W), F32),
        name="rpb_tiles",
    )(rpb.reshape(heads * n_dr, n_dc).astype(F32), jnp.asarray(onehot), jnp.asarray(keep))
    tiles = tiles.reshape(heads, n_dr, GRID_W, GRID_W)

    type_tables, index_of, types = [], {}, []
    for b in range(rows // 2):
        base = int(np.clip(2 * b - NA_KH // 2, 0, rows - NA_KEY_ROWS))
        r = 2 * b + np.arange(2)[:, None]
        kr = base + np.arange(NA_KEY_ROWS)[None, :]
        r0 = np.clip(r - kh // 2, 0, rows - kh)
        valid = (kr >= r0) & (kr < r0 + kh)
        assert (valid.sum(1) == kh).all()
        drow = np.where(valid, kr - r + (NA_KH - 1), -1)
        key = drow.tobytes()
        if key not in index_of:
            index_of[key] = len(type_tables)
            type_tables.append(drow)
        types.append(index_of[key])
    type_tables.append(np.full((2, NA_KEY_ROWS), -1))
    types += [len(type_tables) - 1] * (CTX_LEN // Q_BLOCK)

    masked_tile = jnp.full((heads, GRID_W, GRID_W), MASKED, F32)
    per_type = []
    for drow in type_tables:
        q_rows = [jnp.concatenate([tiles[:, dr] if dr >= 0 else masked_tile for dr in drow[qr]], axis=-1)
                  for qr in range(2)]
        per_type.append(jnp.concatenate(q_rows, axis=1))
    bias = jnp.stack(per_type)
    n_types, nb = bias.shape[0], bias.shape[-1]
    bias = bias.reshape(n_types, heads // 4, 4 * Q_BLOCK, nb)
    return bias, np.asarray(types, np.int32)


def _na_attention(qkv, rpb, *, n_lat):
    n_quads = qkv.shape[1] // (3 * QUAD)
    bias, types = _na_bias_tables(rpb, n_lat)
    return _quad_attention(qkv, bias, types, None, n_lat=n_lat, nb=NA_KEY_ROWS * GRID_W, mode="na", n_slabs=n_quads,
                           q_slab=lambda j: j, k_slab=lambda j: n_quads + j, v_slab=lambda j: 2 * n_quads + j)


def _window_tables(n_lat, nb):
    n_blk = n_lat // Q_BLOCK
    r = (np.arange(4 * Q_BLOCK) % Q_BLOCK)[:, None]
    c = np.arange(nb)[None, :]
    tables, index_of, types = [], {}, []
    for b in range(n_blk):
        d0 = b * Q_BLOCK - int(np.clip((b - 1) * Q_BLOCK, 0, n_lat - nb))
        if d0 not in index_of:
            index_of[d0] = len(tables)
            tables.append(np.where(np.abs(d0 + r - c) <= WINDOW, 0.0, MASKED).astype(np.float32))
        types.append(index_of[d0])
    tables.append(np.full((4 * Q_BLOCK, nb), MASKED, np.float32))
    types += [len(tables) - 1] * (CTX_LEN // Q_BLOCK)
    return np.stack(tables)[:, None], np.asarray(types, np.int32)


def _window_attention(qkv, sink, *, n_lat):
    n_groups = sink.shape[0] // 4
    nb = 3 * Q_BLOCK
    masks, types = _window_tables(n_lat, nb)
    sink_col = jnp.repeat(sink.astype(F32).reshape(4, n_groups).T, Q_BLOCK, axis=1).reshape(n_groups, 4 * Q_BLOCK, 1)
    return _quad_attention(qkv, jnp.asarray(masks), types, sink_col, n_lat=n_lat, nb=nb, mode="win", n_slabs=n_groups,
                           q_slab=lambda g: g, k_slab=lambda g: n_groups, v_slab=lambda g: n_groups + 1)


def _rope_tables(n_lat):
    half = HEAD_DIM // 2
    freqs = ROPE_THETA ** (-jnp.arange(0, half, 2, dtype=F32) / half)
    pos = jnp.arange(n_lat)
    ang_r = (pos // GRID_W).astype(F32)[:, None] * freqs
    ang_c = (pos % GRID_W).astype(F32)[:, None] * freqs
    cos = jnp.concatenate([jnp.cos(ang_r)] * 2 + [jnp.cos(ang_c)] * 2, axis=-1)
    sin = jnp.concatenate([-jnp.sin(ang_r), jnp.sin(ang_r), -jnp.sin(ang_c), jnp.sin(ang_c)], axis=-1)
    cos = jnp.concatenate([jnp.tile(cos, (1, 2)), jnp.ones((CTX_LEN, LANES), F32)], axis=0)
    sin = jnp.concatenate([jnp.tile(sin, (1, 2)), jnp.zeros((CTX_LEN, LANES), F32)], axis=0)
    return cos, sin


def _scale_q(w, q_cols, scale):
    s = jnp.concatenate([jnp.full((q_cols,), scale, F32), jnp.ones((w.shape[1] - q_cols,), F32)])
    return w * s


def kernel(x, c, ctx, c_ctx, ada_w, ada_b, norm_g, ffn_w13, ffn_w2, a_wqkv, a_wo, a_lambda, a_subln,
           b_wqkv, b_wo, b_rpb, c_wqkv, c_wo, c_sink):
    depth = ada_w.shape[0]
    n_lat = x.shape[1]
    d = x.shape[2]
    xa = jnp.concatenate([x[0], ctx[0]], axis=0)
    mods = _mods(c, c_ctx, ada_w, ada_b).reshape(depth, 2, 6, d)
    cos, sin = _rope_tables(n_lat)
    q_scale = HEAD_DIM ** -0.5
    w13_all = ffn_w13.astype(BF16)
    w2_all = ffn_w2.astype(BF16)

    n_heads_c = c_sink.shape[1]
    kv_c = (c_wqkv.shape[2] - n_heads_c * HEAD_DIM) // (2 * HEAD_DIM)
    grp_c = n_heads_c // kv_c
    perm = (np.arange(kv_c)[None, :, None] * grp_c + np.arange(grp_c)[:, None, None]) * HEAD_DIM + np.arange(HEAD_DIM)[None, None, :]
    perm = perm.reshape(-1)

    for i in range(depth):
        mixer, j = i % 3, i // 3
        g = norm_g[i]
        if mixer == 0:
            width = a_wo.shape[1]
            n_heads = width // LANES
            w = _scale_q(a_wqkv[j], width, q_scale * LOG2E).astype(BF16)
            out_slabs = tuple(range(2 * n_heads)) + tuple(2 * n_heads + 2 * h for h in range(n_heads))
            ones_slabs = tuple(2 * n_heads + 2 * h + 1 for h in range(n_heads))
            qkv = _qkv(xa, mods[i], g[0], w, cos, sin, n_lat=n_lat, rope_cols=2 * width, out_slabs=out_slabs,
                       ones_slabs=ones_slabs)
            lam_init = 0.8 - 0.6 * math.exp(-0.3 * i)
            o = _diff_attention(qkv, a_lambda[j], a_subln[j], n_lat=n_lat, n_heads=n_heads, lam_init=lam_init,
                                with_ctx=i < depth - 1)
            wo = a_wo[j]
        elif mixer == 1:
            width = b_wo.shape[1]
            w = _scale_q(b_wqkv[j], width, q_scale * LOG2E).astype(BF16)
            qkv = _qkv(xa, mods[i], g[0], w, cos, sin, n_lat=n_lat, rope_cols=0)
            o = _na_attention(qkv, b_rpb[j], n_lat=n_lat)
            wo = b_wo[j]
        else:
            width = n_heads_c * HEAD_DIM
            wq = c_wqkv[j]
            w = jnp.concatenate([wq[:, :width][:, perm], wq[:, width:]], axis=1)
            w = _scale_q(w, width, q_scale * LOG2E).astype(BF16)
            qkv = _qkv(xa, mods[i], g[0], w, cos, sin, n_lat=n_lat, rope_cols=width + kv_c * HEAD_DIM)
            o = _window_attention(qkv, c_sink[j], n_lat=n_lat)
            wo = c_wo[j][perm, :]
        out_rows = n_lat if i == depth - 1 else xa.shape[0]
        xa = _post_attn(o, xa, mods[i], g, wo.astype(BF16), w13_all, w2_all, i, n_lat=n_lat, out_rows=out_rows)
    return xa[None]
```
